```python
import math
import jax
import jax.numpy as jnp
from jax import lax
import numpy as np

D_MODEL = 2048
BATCH = 8
SEQ = 4096
DEPTH = 4

GRID_W = 64
CTX_LEN = 256
RET_HEADS = 4
RET_DK = 128
RET_DV = 256
RET_CHUNK = 128
DIFF_HEADS = 4
DIFF_HD = 64
DIFF_DV = 2 * DIFF_HD
Q_BLOCK = 128
NA_HEADS = 8
NA_HD = 64
NA_KR = 8
NA_KW = 16
N_EXPERTS = 16
N_GROUPS = 4
EXPERTS_PER_GROUP = N_EXPERTS // N_GROUPS
TOP_K = 2
D_EXPERT = D_MODEL // 2
N_BRANCH = 3
ROPE_BASE = 10000.0
EPS = 1e-6

RET_QK_W = RET_HEADS * RET_DK
RET_V_W = RET_HEADS * RET_DV
DIFF_QK_W = DIFF_HEADS * 2 * DIFF_HD
DIFF_V_W = DIFF_HEADS * DIFF_DV
NA_W = NA_HEADS * NA_HD
SPLIT_SIZES = (RET_QK_W, RET_QK_W, RET_V_W, RET_V_W, DIFF_QK_W, DIFF_QK_W, DIFF_V_W, NA_W, NA_W, NA_W, N_BRANCH * D_MODEL)
P_TOTAL = 2 * RET_QK_W + 2 * RET_V_W + 2 * DIFF_QK_W + DIFF_V_W + 3 * NA_W + N_BRANCH * D_MODEL

kernel_name = 'hybrid_gated_retention_diffattn_natten_moe_dit'

F32 = jnp.float32


def rms_norm(x, g):
    xf = x.astype(F32)
    y = xf * lax.rsqrt(jnp.mean(xf * xf, axis=-1, keepdims=True) + EPS)
    return (y * g.astype(F32)).astype(x.dtype)


def modulate(x, g, shift, scale):
    return rms_norm(x, g) * (1.0 + scale) + shift


def _heads(a, *shape):
    return a.reshape(a.shape[:2] + shape)


def rope_1d(x, pos):
    d = x.shape[-1]
    inv_freq = jnp.power(ROPE_BASE, -jnp.arange(0, d, 2, dtype=F32) / d)
    ang = pos.astype(F32)[:, None] * inv_freq[None, :]
    bshape = (pos.shape[0],) + (1,) * (x.ndim - 3) + (d // 2,)
    cos = jnp.cos(ang).reshape(bshape).astype(x.dtype)
    sin = jnp.sin(ang).reshape(bshape).astype(x.dtype)
    x1, x2 = x[..., : d // 2], x[..., d // 2:]
    return jnp.concatenate([x1 * cos - x2 * sin, x2 * cos + x1 * sin], axis=-1)


def rope_2d(x, row, col):
    h = x.shape[-1] // 2
    return jnp.concatenate([rope_1d(x[..., :h], row), rope_1d(x[..., h:], col)], axis=-1)


def retention_chunkwise(q, k, v, log_gamma, s0):
    B, T, H, dk = q.shape
    dv = v.shape[-1]
    C = RET_CHUNK
    n = T // C
    to_chunks = lambda a: a.reshape(B, n, C, H, a.shape[-1]).transpose(1, 0, 3, 2, 4)
    idx = jnp.arange(C, dtype=F32)
    dist = idx[:, None] - idx[None, :]
    lg = log_gamma.astype(F32)
    decay_in = jnp.where(dist >= 0, jnp.exp(lg[:, None, None] * jnp.maximum(dist, 0.0)), 0.0)
    q_decay = jnp.exp(lg[:, None] * (idx + 1.0))[..., None]
    k_decay = jnp.exp(lg[:, None] * (C - 1.0 - idx))[..., None]
    chunk_decay = jnp.exp(lg * C)[:, None, None]

    def step(s, inp):
        qi, ki, vi = (a.astype(F32) for a in inp)
        att = jnp.einsum('bhid,bhjd->bhij', qi, ki) * decay_in
        o = jnp.einsum('bhij,bhjv->bhiv', att, vi) + jnp.einsum('bhid,bhdv->bhiv', qi * q_decay, s)
        s = s * chunk_decay + jnp.einsum('bhjd,bhjv->bhdv', ki * k_decay, vi)
        return s, o

    s, o = lax.scan(step, s0, (to_chunks(q), to_chunks(k), to_chunks(v)))
    o = o.transpose(1, 0, 3, 2, 4).reshape(B, T, H, dv)
    return o.astype(q.dtype), s


def bidirectional_retention(qc, kc, vc, ql, kl, vl, decay_logit):
    log_gamma = jax.nn.log_sigmoid(decay_logit.astype(F32))
    B, _, H, dk = qc.shape
    s0 = jnp.zeros((B, H, dk, vc.shape[-1]), F32)
    flip = lambda a: a[:, ::-1]
    oc_f, sc_f = retention_chunkwise(qc, kc, vc, log_gamma[0], s0)
    oc_b, sc_b = retention_chunkwise(flip(qc), flip(kc), flip(vc), log_gamma[1], s0)
    ol_f, _ = retention_chunkwise(ql, kl, vl, log_gamma[0], sc_f)
    ol_b, _ = retention_chunkwise(flip(ql), flip(kl), flip(vl), log_gamma[1], sc_b)
    return oc_f + flip(oc_b), ol_f + flip(ol_b)


def _two_map_attention(q, k, v, lam):
    s = jnp.einsum('bqhmd,bkhmd->bhmqk', q, k, preferred_element_type=F32) * (q.shape[-1] ** -0.5)
    p = jax.nn.softmax(s, axis=-1)
    a = (p[:, :, 0] - lam * p[:, :, 1]).astype(v.dtype)
    return jnp.einsum('bhqk,bkhv->bqhv', a, v)


def differential_attention(qc, kc, vc, ql, kl, vl, lam, ctx_out):
    B, S = ql.shape[:2]
    nb = S // Q_BLOCK
    k_all = jnp.concatenate([kl, kc], axis=1)
    v_all = jnp.concatenate([vl, vc], axis=1)
    qb = jnp.moveaxis(ql.reshape((B, nb, Q_BLOCK) + ql.shape[2:]), 1, 0)
    ol = lax.map(lambda qi: _two_map_attention(qi, k_all, v_all, lam), qb)
    ol = jnp.moveaxis(ol, 0, 1).reshape((B, S) + vl.shape[2:])
    oc = _two_map_attention(qc, kc, vc, lam) if ctx_out else None
    return oc, ol


def dense_attention(q, k, v):
    s = jnp.einsum('bqhd,bkhd->bhqk', q, k, preferred_element_type=F32) * (q.shape[-1] ** -0.5)
    p = jax.nn.softmax(s, axis=-1).astype(v.dtype)
    return jnp.einsum('bhqk,bkhd->bqhd', p, v)


def neighbourhood_attention(qc, kc, vc, ql, kl, vl, rpb, ctx_out):
    B, S, H, d = ql.shape
    rows = S // GRID_W
    kr = min(NA_KR, rows)
    scale = d ** -0.5
    qg = ql.reshape(B, rows, GRID_W, H, d)
    kg = kl.reshape(B, rows, GRID_W, H, d)
    vg = vl.reshape(B, rows, GRID_W, H, d)
    col = jnp.arange(GRID_W)
    col_idx = jnp.clip(col - NA_KW // 2, 0, GRID_W - NA_KW)[:, None] + jnp.arange(NA_KW)[None, :]
    col_off = col_idx - col[:, None] + (NA_KW - 1)
    n_loc = kr * NA_KW

    def row_block(r):
        r0 = jnp.clip(r - kr // 2, 0, rows - kr)
        q_r = lax.dynamic_index_in_dim(qg, r, axis=1, keepdims=False)
        k_r = lax.dynamic_slice_in_dim(kg, r0, kr, axis=1)[:, :, col_idx]
        v_r = lax.dynamic_slice_in_dim(vg, r0, kr, axis=1)[:, :, col_idx]
        row_off = r0 + jnp.arange(kr) - r + (NA_KR - 1)
        bias = rpb[:, row_off][:, :, col_off].transpose(0, 2, 1, 3).astype(F32)
        s_loc = jnp.einsum('bchd,bicjhd->bhcij', q_r, k_r, preferred_element_type=F32) * scale + bias[None]
        s_ctx = jnp.einsum('bchd,bnhd->bhcn', q_r, kc, preferred_element_type=F32) * scale
        s = jnp.concatenate([s_loc.reshape(B, H, GRID_W, n_loc), s_ctx], axis=-1)
        p = jax.nn.softmax(s, axis=-1).astype(vl.dtype)
        p_loc = p[..., :n_loc].reshape(B, H, GRID_W, kr, NA_KW)
        return (jnp.einsum('bhcij,bicjhd->bchd', p_loc, v_r)
                + jnp.einsum('bhcn,bnhd->bchd', p[..., n_loc:], vc))

    ol = lax.map(row_block, jnp.arange(rows))
    ol = ol.transpose(1, 0, 2, 3, 4).reshape(B, S, H, d)
    oc = dense_attention(qc, kc, vc) if ctx_out else None
    return oc, ol


def mixer_sublayer(hc, hl, w_in, ret_logit, ret_g, dq_g, dk_g, d_lam, d_g, nq_g, nk_g, rpb,
                   w_ret_o, w_diff_o, w_na_o, w_out, lam_init, ctx_out):
    C, S = hc.shape[1], hl.shape[1]
    points = np.cumsum(SPLIT_SIZES)[:-1].tolist()
    z = jnp.concatenate([hc, hl], axis=1) @ w_in
    rq, rk, rv, rg, dq, dk, dv, nq, nk, nv, gates = jnp.split(z, points, axis=-1)
    t = jnp.arange(S)
    row, col = t // GRID_W, t % GRID_W
    sides = lambda a: (a[:, :C], a[:, C:])
    keep = (lambda a: a) if ctx_out else (lambda a: a[:, C:])
    merge = (lambda oc, ol: jnp.concatenate([oc, ol], axis=1)) if ctx_out else (lambda oc, ol: ol)

    rq = _heads(rq, RET_HEADS, RET_DK)
    rk = _heads(rk, RET_HEADS, RET_DK) * (RET_DK ** -0.5)
    rv = _heads(rv, RET_HEADS, RET_DV)
    (rqc, rql), (rkc, rkl), (rvc, rvl) = sides(rq), sides(rk), sides(rv)
    rql, rkl = rope_2d(rql, row, col), rope_2d(rkl, row, col)
    rc, rl = bidirectional_retention(rqc, rkc, rvc, rql, rkl, rvl, ret_logit)
    r = rms_norm(merge(rc, rl), ret_g.reshape(RET_HEADS, RET_DV))
    y_ret = (jax.nn.silu(keep(rg)) * r.reshape(r.shape[:2] + (RET_V_W,))) @ w_ret_o

    dq = rms_norm(_heads(dq, DIFF_HEADS, 2, DIFF_HD), dq_g)
    dk = rms_norm(_heads(dk, DIFF_HEADS, 2, DIFF_HD), dk_g)
    dv = _heads(dv, DIFF_HEADS, DIFF_DV)
    (dqc, dql), (dkc, dkl), (dvc, dvl) = sides(dq), sides(dk), sides(dv)
    dql, dkl = rope_2d(dql, row, col), rope_2d(dkl, row, col)
    lam_p = d_lam.astype(F32)
    lam = jnp.exp(jnp.sum(lam_p[0] * lam_p[1])) - jnp.exp(jnp.sum(lam_p[2] * lam_p[3])) + lam_init
    oc, ol = differential_attention(dqc, dkc, dvc, dql, dkl, dvl, lam, ctx_out)
    o = rms_norm(merge(oc, ol), d_g.reshape(DIFF_HEADS, DIFF_DV)) * (1.0 - lam_init)
    y_diff = o.reshape(o.shape[:2] + (DIFF_V_W,)) @ w_diff_o

    nq = rms_norm(_heads(nq, NA_HEADS, NA_HD), nq_g)
    nk = rms_norm(_heads(nk, NA_HEADS, NA_HD), nk_g)
    nv = _heads(nv, NA_HEADS, NA_HD)
    (nqc, nql), (nkc, nkl), (nvc, nvl) = sides(nq), sides(nk), sides(nv)
    oc, ol = neighbourhood_attention(nqc, nkc, nvc, nql, nkl, nvl, rpb, ctx_out)
    o = merge(oc, ol)
    y_na = o.reshape(o.shape[:2] + (NA_W,)) @ w_na_o

    g_ret, g_diff, g_na = jnp.split(jax.nn.sigmoid(keep(gates)), N_BRANCH, axis=-1)
    return (g_ret * y_ret + g_diff * y_diff + g_na * y_na) @ w_out


def grouped_moe(h, w_router, b_router, w_gate, w_up, w_down):
    shape = h.shape
    hf = h.reshape(-1, shape[-1])
    s = jax.nn.sigmoid(jnp.dot(hf, w_router, preferred_element_type=F32))
    sel = s + b_router.astype(F32)
    grp_score = lax.top_k(sel.reshape(-1, N_GROUPS, EXPERTS_PER_GROUP), 2)[0].sum(-1)
    g_idx = jnp.argmax(grp_score, axis=-1)
    in_group = (jnp.arange(N_EXPERTS) // EXPERTS_PER_GROUP)[None, :] == g_idx[:, None]
    _, e_idx = lax.top_k(jnp.where(in_group, sel, -jnp.inf), TOP_K)
    w = jnp.take_along_axis(s, e_idx, axis=-1)
    w = w / jnp.sum(w, axis=-1, keepdims=True)
    gates = jnp.einsum('nke,nk->ne', jax.nn.one_hot(e_idx, N_EXPERTS, dtype=F32), w).astype(h.dtype)
    y = jnp.zeros_like(hf)
    for e in range(N_EXPERTS):
        a = jax.nn.silu(hf @ w_gate[e]) * (hf @ w_up[e])
        y = y + gates[:, e:e + 1] * (a @ w_down[e])
    return y.reshape(shape)


def setup_inputs(seed: int = 0) -> dict:
    key = jax.random.key(seed)
    ks = jax.random.split(key, 32)
    L, D = DEPTH, D_MODEL
    nrm = lambda k, shape, s: jax.random.normal(k, shape, F32) * s
    base_logit = jnp.log(jnp.exp2(5.0 + jnp.arange(RET_HEADS, dtype=F32)) - 1.0)
    return {
        'x': nrm(ks[0], (BATCH, SEQ, D), 1.0),
        'c': nrm(ks[1], (BATCH, D), 1.0),
        'ctx': nrm(ks[2], (BATCH, CTX_LEN, D), 1.0),
        'c_ctx': nrm(ks[3], (D,), 1.0),
        'w_ada': nrm(ks[4], (L, D, 6 * D), 0.5 * D ** -0.5),
        'b_ada': nrm(ks[5], (L, 6 * D), 0.01),
        'norm1_g': 1.0 + nrm(ks[6], (L, D), 0.05),
        'norm2_g': 1.0 + nrm(ks[7], (L, D), 0.05),
        'w_in': nrm(ks[8], (L, D, P_TOTAL), D ** -0.5),
        'ret_decay_logit': base_logit + nrm(ks[9], (L, 2, RET_HEADS), 0.1),
        'ret_norm_g': 1.0 + nrm(ks[10], (L, RET_V_W), 0.05),
        'diff_q_norm_g': 1.0 + nrm(ks[11], (L, DIFF_HD), 0.05),
        'diff_k_norm_g': 1.0 + nrm(ks[12], (L, DIFF_HD), 0.05),
        'diff_lambda': nrm(ks[13], (L, 4, DIFF_HD), 0.1),
        'diff_norm_g': 1.0 + nrm(ks[14], (L, DIFF_V_W), 0.05),
        'na_q_norm_g': 1.0 + nrm(ks[15], (L, NA_HD), 0.05),
        'na_k_norm_g': 1.0 + nrm(ks[16], (L, NA_HD), 0.05),
        'na_rpb': nrm(ks[17], (L, NA_HEADS, 2 * NA_KR - 1, 2 * NA_KW - 1), 0.1),
        'w_ret_out': nrm(ks[18], (L, RET_V_W, D), RET_V_W ** -0.5),
        'w_diff_out': nrm(ks[19], (L, DIFF_V_W, D), DIFF_V_W ** -0.5),
        'w_na_out': nrm(ks[20], (L, NA_W, D), NA_W ** -0.5),
        'w_out': nrm(ks[21], (L, D, D), D ** -0.5),
        'w_router': nrm(ks[22], (D, N_EXPERTS), D ** -0.5),
        'b_router': nrm(ks[23], (N_EXPERTS,), 0.01),
        'w_exp_gate': nrm(ks[24], (L, N_EXPERTS, D, D_EXPERT), D ** -0.5),
        'w_exp_up': nrm(ks[25], (L, N_EXPERTS, D, D_EXPERT), D ** -0.5),
        'w_exp_down': nrm(ks[26], (L, N_EXPERTS, D_EXPERT, D), D_EXPERT ** -0.5),
    }


def reference(x, c, ctx, c_ctx, w_ada, b_ada, norm1_g, norm2_g, w_in, ret_decay_logit, ret_norm_g,
              diff_q_norm_g, diff_k_norm_g, diff_lambda, diff_norm_g, na_q_norm_g, na_k_norm_g, na_rpb,
              w_ret_out, w_diff_out, w_na_out, w_out, w_router, b_router, w_exp_gate, w_exp_up, w_exp_down):
    C = ctx.shape[1]
    c_act = jax.nn.silu(c)
    cc_act = jax.nn.silu(c_ctx)
    xc, xl = ctx, x
    for l in range(DEPTH):
        last = l == DEPTH - 1
        lam_init = 0.8 - 0.6 * math.exp(-0.3 * l)
        ml = jnp.split((c_act @ w_ada[l] + b_ada[l])[:, None, :], 6, axis=-1)
        mc = jnp.split((cc_act @ w_ada[l] + b_ada[l])[None, None, :], 6, axis=-1)
        hc = modulate(xc, norm1_g[l], mc[0], mc[1])
        hl = modulate(xl, norm1_g[l], ml[0], ml[1])
        y = mixer_sublayer(hc, hl, w_in[l], ret_decay_logit[l], ret_norm_g[l], diff_q_norm_g[l],
                           diff_k_norm_g[l], diff_lambda[l], diff_norm_g[l], na_q_norm_g[l], na_k_norm_g[l],
                           na_rpb[l], w_ret_out[l], w_diff_out[l], w_na_out[l], w_out[l], lam_init,
                           not last)
        if last:
            xl = xl + ml[2] * y
            h2 = modulate(xl, norm2_g[l], ml[3], ml[4])
            xl = xl + ml[5] * grouped_moe(h2, w_router, b_router, w_exp_gate[l], w_exp_up[l], w_exp_down[l])
        else:
            xc = xc + mc[2] * y[:, :C]
            xl = xl + ml[2] * y[:, C:]
            h2 = jnp.concatenate([modulate(xc, norm2_g[l], mc[3], mc[4]),
                                  modulate(xl, norm2_g[l], ml[3], ml[4])], axis=1)
            f = grouped_moe(h2, w_router, b_router, w_exp_gate[l], w_exp_up[l], w_exp_down[l])
            xc = xc + mc[5] * f[:, :C]
            xl = xl + ml[5] * f[:, C:]
    return xl
```

```python
import functools
import math

import numpy as np
import jax
import jax.numpy as jnp
from jax import lax
from jax.experimental import pallas as pl
from jax.experimental.pallas import tpu as pltpu

F32 = jnp.float32
BF16 = jnp.bfloat16
I32 = jnp.int32

GRID_W = 64
RET_HEADS, RET_DK, RET_DV, RET_CHUNK = 4, 128, 256, 128
DIFF_HEADS, DIFF_HD, DIFF_DV = 4, 64, 128
NA_HEADS, NA_HD, NA_KR, NA_KW = 8, 64, 8, 16
N_EXPERTS, N_GROUPS = 16, 4
EXPERTS_PER_GROUP = N_EXPERTS // N_GROUPS
ROPE_BASE = 10000.0
EPS = 1e-6
NEG = -1e30

SEG = 512
C_RQ, C_RK, C_RV, C_RG, C_DQ, C_DK, C_DV, C_NQ, C_NK, C_NV, C_GATES = 0, 1, 2, 4, 6, 7, 8, 9, 10, 11, 12
NA_QROWS = 8
NA_KROWS = 16

V7X_VMEM_BYTES = 64 * 1024 * 1024
VMEM_LIMIT = V7X_VMEM_BYTES - 8 * 1024 * 1024


def _cparams(sem):
    return pltpu.CompilerParams(dimension_semantics=sem, vmem_limit_bytes=VMEM_LIMIT)


def _sigmoid(x):
    return 1.0 / (1.0 + jnp.exp(-x))


def _tile(n, pref):
    t = min(n, pref)
    while n % t:
        t //= 2
    return t


def _dot_nt(a, b):
    return lax.dot_general(a, b, (((1,), (1,)), ((), ())), preferred_element_type=F32)


def _dot_tn(a, b):
    return lax.dot_general(a, b, (((0,), (0,)), ((), ())), preferred_element_type=F32)


def _dot(a, b):
    return jnp.dot(a, b, preferred_element_type=F32)


def _ada_kernel(c_ref, w_ref, b_ref, o_ref):
    c = c_ref[...]
    a = (c * _sigmoid(c)).astype(BF16)
    o_ref[0] = _dot(a, w_ref[0].astype(BF16)) + b_ref[0]


def _ada(c_all, w_ada, b_ada):
    L, D, N = w_ada.shape
    M = c_all.shape[0]
    tn = _tile(N, 1024)
    return pl.pallas_call(
        _ada_kernel,
        out_shape=jax.ShapeDtypeStruct((L, M, N), F32),
        grid=(L, N // tn),
        in_specs=[
            pl.BlockSpec((M, D), lambda l, j: (0, 0)),
            pl.BlockSpec((1, D, tn), lambda l, j: (l, 0, j)),
            pl.BlockSpec((1, 1, tn), lambda l, j: (l, 0, j)),
        ],
        out_specs=pl.BlockSpec((1, M, tn), lambda l, j: (l, 0, j)),
        compiler_params=_cparams(("parallel", "parallel")),
        name="ada",
    )(c_all, w_ada, b_ada.reshape(L, 1, N))


def _modnorm(x, g, shift, scale):
    ms = jnp.mean(x * x, axis=-1, keepdims=True)
    y = x * lax.rsqrt(ms + EPS) * g
    return y * (1.0 + scale) + shift


def _inproj_kernel(x_ref, g_ref, sh_ref, sc_ref, w_ref, o_ref, h_ref):
    @pl.when(pl.program_id(1) == 0)
    def _():
        h_ref[...] = _modnorm(x_ref[...], g_ref[...], sh_ref[0], sc_ref[0]).astype(BF16)

    o_ref[...] = _dot(h_ref[...], w_ref[...]).astype(o_ref.dtype)


def _mod_index(i, n_lat_tiles, tiles_per_batch, n_batch):
    return jnp.where(i < n_lat_tiles, i // tiles_per_batch, n_batch)


def _inproj(x, g, shift, scale, w, dims):
    R, D = x.shape
    P = w.shape[1]
    tm = _tile(dims["BC"], 1024)
    tn = _tile(P, 1024)
    nlt, tpb, B = dims["BS"] // tm, dims["S"] // tm, dims["B"]
    mod = lambda i, j: (_mod_index(i, nlt, tpb, B), 0, 0)
    return pl.pallas_call(
        _inproj_kernel,
        out_shape=jax.ShapeDtypeStruct((R, P), BF16),
        grid=(R // tm, P // tn),
        in_specs=[
            pl.BlockSpec((tm, D), lambda i, j: (i, 0)),
            pl.BlockSpec((1, D), lambda i, j: (0, 0)),
            pl.BlockSpec((1, 1, D), mod),
            pl.BlockSpec((1, 1, D), mod),
            pl.BlockSpec((D, tn), lambda i, j: (0, j)),
        ],
        out_specs=pl.BlockSpec((tm, tn), lambda i, j: (i, j)),
        scratch_shapes=[pltpu.VMEM((tm, D), BF16)],
        compiler_params=_cparams(("parallel", "arbitrary")),
        name="inproj",
    )(x, g, shift, scale, w)


def _group_mean_sq(x, gmat):
    xx = x * x
    hi = xx.astype(BF16)
    lo = (xx - hi.astype(F32)).astype(BF16)
    return _dot(hi, gmat) + _dot(lo, gmat)


def _rope(x, cos, sin, half):
    w = x.shape[-1]
    lane = lax.broadcasted_iota(I32, x.shape, 1)
    first = (lane % (2 * half)) < half
    swapped = jnp.where(first, pltpu.roll(x, w - half, 1), pltpu.roll(x, half, 1))
    return x * cos + swapped * sin


def _prep_kernel(rq_ref, rk_ref, dq_ref, dk_ref, nq_ref, nk_ref, gains_ref, gmat_ref, tab_ref,
                 orq_ref, ork_ref, odq_ref, odk_ref, onq_ref, onk_ref):
    reps = SEG // 128
    cos_r = jnp.concatenate([tab_ref[0]] * reps, axis=1)
    sin_r = jnp.concatenate([tab_ref[1]] * reps, axis=1)
    cos_d = jnp.concatenate([tab_ref[2]] * reps, axis=1)
    sin_d = jnp.concatenate([tab_ref[3]] * reps, axis=1)
    gmat = gmat_ref[...]

    def normed(ref, row):
        x = ref[...].astype(F32)
        return x * lax.rsqrt(_group_mean_sq(x, gmat) + EPS) * gains_ref[row:row + 1, :]

    orq_ref[...] = _rope(rq_ref[...].astype(F32), cos_r, sin_r, RET_DK // 4).astype(BF16)
    ork_ref[...] = _rope(rk_ref[...].astype(F32) * (RET_DK ** -0.5), cos_r, sin_r, RET_DK // 4).astype(BF16)
    odq_ref[...] = (_rope(normed(dq_ref, 0), cos_d, sin_d, DIFF_HD // 4) * (DIFF_HD ** -0.5)).astype(BF16)
    odk_ref[...] = _rope(normed(dk_ref, 1), cos_d, sin_d, DIFF_HD // 4).astype(BF16)
    onq_ref[...] = (normed(nq_ref, 2) * (NA_HD ** -0.5)).astype(BF16)
    onk_ref[...] = normed(nk_ref, 3).astype(BF16)


def _prep(z, gains, gmat, tables, dims):
    R = z.shape[0]
    tm = _tile(dims["BC"], 512)
    nlt, tpb = dims["BS"] // tm, dims["S"] // tm
    zspec = lambda c: pl.BlockSpec((tm, SEG), lambda i: (i, c))
    tab = lambda i: (0, jnp.where(i < nlt, i % tpb, tpb), 0)
    ospec = pl.BlockSpec((tm, SEG), lambda i: (i, 0))
    return pl.pallas_call(
        _prep_kernel,
        out_shape=[jax.ShapeDtypeStruct((R, SEG), BF16)] * 6,
        grid=(R // tm,),
        in_specs=[zspec(C_RQ), zspec(C_RK), zspec(C_DQ), zspec(C_DK), zspec(C_NQ), zspec(C_NK),
                  pl.BlockSpec((8, SEG), lambda i: (0, 0)),
                  pl.BlockSpec((SEG, SEG), lambda i: (0, 0)),
                  pl.BlockSpec((4, tm, 128), tab)],
        out_specs=[ospec] * 6,
        compiler_params=_cparams(("parallel",)),
        name="prep",
    )(z, z, z, z, z, z, gains, gmat, tables)


def _rope_tables(S, tm):
    t = jnp.arange(S)
    row, col = t // GRID_W, t % GRID_W

    def one(d, width):
        inv = jnp.power(ROPE_BASE, -jnp.arange(0, d, 2, dtype=F32) / d)
        ar = row.astype(F32)[:, None] * inv[None, :]
        ac = col.astype(F32)[:, None] * inv[None, :]
        cos = jnp.concatenate([jnp.cos(ar), jnp.cos(ar), jnp.cos(ac), jnp.cos(ac)], axis=-1)
        sin = jnp.concatenate([-jnp.sin(ar), jnp.sin(ar), -jnp.sin(ac), jnp.sin(ac)], axis=-1)
        reps = width // cos.shape[-1]
        return jnp.tile(cos, (1, reps)), jnp.tile(sin, (1, reps))

    cr, sr = one(RET_DK // 2, 128)
    cd, sd = one(DIFF_HD // 2, 128)
    tab = jnp.stack([cr, sr, cd, sd])
    ident = jnp.stack([jnp.ones((tm, 128), F32), jnp.zeros((tm, 128), F32)] * 2)
    return jnp.concatenate([tab, ident], axis=1)


def _ret_kernel(lg_ref, qf_ref, kf_ref, vf_ref, qb_ref, kb_ref, vb_ref, of_ref, ob_ref, s_ref):
    C = RET_CHUNK

    @pl.when(pl.program_id(1) == 0)
    def _():
        s_ref[...] = jnp.zeros_like(s_ref)

    ii = lax.broadcasted_iota(I32, (C, C), 0).astype(F32)
    jj = lax.broadcasted_iota(I32, (C, C), 1).astype(F32)
    ri = lax.broadcasted_iota(I32, (C, RET_DK), 0).astype(F32)

    for h in range(RET_HEADS):
        qs = slice(h * RET_DK, (h + 1) * RET_DK)
        vs = slice(h * RET_DV, (h + 1) * RET_DV)
        for d, (q_ref, k_ref, v_ref, o_ref) in enumerate(((qf_ref, kf_ref, vf_ref, of_ref),
                                                         (qb_ref, kb_ref, vb_ref, ob_ref))):
            lg = lg_ref[d, h]
            q, k, v = q_ref[:, qs], k_ref[:, qs], v_ref[:, vs]
            if d == 0:
                dist = ii - jj
                q_dec = jnp.exp(lg * (ri + 1.0))
                k_dec = jnp.exp(lg * (C - 1.0 - ri))
            else:
                dist = jj - ii
                q_dec = jnp.exp(lg * (C - ri))
                k_dec = jnp.exp(lg * ri)
            decay = jnp.where(dist >= 0, jnp.exp(lg * jnp.maximum(dist, 0.0)), 0.0)
            att = (_dot_nt(q, k) * decay).astype(BF16)
            s = s_ref[d, h]
            o = _dot(att, v) + _dot((q.astype(F32) * q_dec).astype(BF16), s.astype(BF16))
            s_ref[d, h] = s * jnp.exp(lg * C) + _dot_tn((k.astype(F32) * k_dec).astype(BF16), v)
            o_ref[:, vs] = o.astype(o_ref.dtype)


def _retention(lg, rq, rk, z, dims):
    B, S, C, BS = dims["B"], dims["S"], dims["C"], dims["BS"]
    R = rq.shape[0]
    nc, nl = C // RET_CHUNK, S // RET_CHUNK
    base = BS // RET_CHUNK
    fwd = lambda b, c: jnp.where(c < nc, base + b * nc + c, b * nl + (c - nc))
    bwd = lambda b, c: jnp.where(c < nc, base + b * nc + (nc - 1 - c), b * nl + (nl - 1 - (c - nc)))
    vw = RET_HEADS * RET_DV
    spec = lambda w, col, f: pl.BlockSpec((RET_CHUNK, w), lambda b, c: (f(b, c), col))
    return pl.pallas_call(
        _ret_kernel,
        out_shape=[jax.ShapeDtypeStruct((R, vw), BF16), jax.ShapeDtypeStruct((R, vw), BF16)],
        grid=(B, nc + nl),
        in_specs=[pl.BlockSpec(memory_space=pltpu.SMEM),
                  spec(SEG, 0, fwd), spec(SEG, 0, fwd), spec(vw, C_RV * SEG // vw, fwd),
                  spec(SEG, 0, bwd), spec(SEG, 0, bwd), spec(vw, C_RV * SEG // vw, bwd)],
        out_specs=[spec(vw, 0, fwd), spec(vw, 0, bwd)],
        scratch_shapes=[pltpu.VMEM((2, RET_HEADS, RET_DK, RET_DV), F32)],
        compiler_params=_cparams(("parallel", "arbitrary")),
        name="retention",
    )(lg, rq, rk, z, rq, rk, z)


def _split_halves(q):
    lane = lax.broadcasted_iota(I32, q.shape, 1)
    lo = jnp.where(lane < 64, q, jnp.zeros_like(q))
    hi = jnp.where(lane >= 64, q, jnp.zeros_like(q))
    return jnp.concatenate([lo, hi], axis=0)


def _softmax_parts(parts):
    m = parts[0].max(axis=-1, keepdims=True)
    for p in parts[1:]:
        m = jnp.maximum(m, p.max(axis=-1, keepdims=True))
    es = [jnp.exp(p - m) for p in parts]
    den = es[0].sum(axis=-1, keepdims=True)
    for e in es[1:]:
        den = den + e.sum(axis=-1, keepdims=True)
    return es, 1.0 / den


def _diff_kernel(lam_ref, q_ref, *refs, has_latent):
    if has_latent:
        kl_ref, vl_ref, kc_ref, vc_ref, o_ref = refs
        kv = ((kl_ref, vl_ref), (kc_ref, vc_ref))
    else:
        kc_ref, vc_ref, o_ref = refs
        kv = ((kc_ref, vc_ref),)
    lam = lam_ref[0, 0]
    tq = q_ref.shape[0]
    for h in range(DIFF_HEADS):
        hs = slice(h * 128, (h + 1) * 128)
        q2 = _split_halves(q_ref[:, hs])
        es, rden = _softmax_parts([_dot_nt(q2, k_ref[:, hs]) for k_ref, _ in kv])
        c1, c2 = rden[:tq], lam * rden[tq:]
        o = None
        for e, (_, v_ref) in zip(es, kv):
            a = (e[:tq] * c1 - e[tq:] * c2).astype(BF16)
            t = _dot(a, v_ref[:, hs])
            o = t if o is None else o + t
        o_ref[:, hs] = o.astype(o_ref.dtype)


def _diff_attention(lam, dq, dk, z, dims, latent):
    B, S, C, BS = dims["B"], dims["S"], dims["C"], dims["BS"]
    lat_k = pl.BlockSpec((S, SEG), lambda b, i: (b, 0))
    lat_v = pl.BlockSpec((S, SEG), lambda b, i: (b, C_DV))
    ctx_k = pl.BlockSpec((C, SEG), lambda b, i: (BS // C + b, 0))
    ctx_v = pl.BlockSpec((C, SEG), lambda b, i: (BS // C + b, C_DV))
    if latent:
        tq = _tile(S, 256)
        nq = S // tq
        qmap = omap = lambda b, i: (b * nq + i, 0)
        in_specs = [lat_k, lat_v, ctx_k, ctx_v]
        args = (dk, z, dk, z)
    else:
        tq, nq = C, 1
        qmap = lambda b, i: (BS // C + b, 0)
        omap = lambda b, i: (b, 0)
        in_specs = [ctx_k, ctx_v]
        args = (dk, z)
    return pl.pallas_call(
        functools.partial(_diff_kernel, has_latent=latent),
        out_shape=jax.ShapeDtypeStruct((B * nq * tq, SEG), BF16),
        grid=(B, nq),
        in_specs=[pl.BlockSpec(memory_space=pltpu.SMEM), pl.BlockSpec((tq, SEG), qmap)] + in_specs,
        out_specs=pl.BlockSpec((tq, SEG), omap),
        compiler_params=_cparams(("parallel", "arbitrary")),
        name="diff_lat" if latent else "diff_ctx",
    )(lam, dq, *args)


def _na_kernel(q_ref, *refs, has_local, rows):
    if has_local:
        kl_ref, vl_ref, kc_ref, vc_ref, bias_ref, o_ref = refs
        qb = pl.program_id(1)
        kr0 = jnp.clip(qb * NA_QROWS - (NA_KROWS - NA_QROWS) // 2, 0, rows - NA_KROWS)
        k0 = pl.multiple_of(kr0 * GRID_W, 256)
        nk = NA_KROWS * GRID_W
    else:
        kc_ref, vc_ref, o_ref = refs
    tq = q_ref.shape[0]
    lane = lax.broadcasted_iota(I32, (tq, 128), 1)
    for p in range(NA_HEADS // 2):
        ps = slice(p * 128, (p + 1) * 128)
        q2 = _split_halves(q_ref[:, ps])
        parts = [_dot_nt(q2, kc_ref[:, ps])]
        vals = [vc_ref[:, ps]]
        if has_local:
            s_loc = _dot_nt(q2, kl_ref[pl.ds(k0, nk), ps])
            bias = jnp.concatenate([bias_ref[0, 2 * p], bias_ref[0, 2 * p + 1]], axis=0).astype(F32)
            parts.append(s_loc + bias)
            vals.append(vl_ref[pl.ds(k0, nk), ps])
        es, rden = _softmax_parts(parts)
        o = None
        for e, v in zip(es, vals):
            t = _dot((e * rden).astype(BF16), v)
            o = t if o is None else o + t
        o_ref[:, ps] = jnp.where(lane < 64, o[:tq], o[tq:]).astype(o_ref.dtype)


def _na_attention(nq_, nk_, z, bias, dims, latent):
    B, S, C, BS = dims["B"], dims["S"], dims["C"], dims["BS"]
    ctx_k = pl.BlockSpec((C, SEG), lambda b, i: (BS // C + b, 0))
    ctx_v = pl.BlockSpec((C, SEG), lambda b, i: (BS // C + b, C_NV))
    if latent:
        tq = NA_QROWS * GRID_W
        nq = S // tq
        qmap = omap = lambda b, i: (b * nq + i, 0)
        btype = lambda b, i: (jnp.where(i == 0, 0, jnp.where(i == nq - 1, 2, 1)), 0, 0, 0)
        in_specs = [pl.BlockSpec((S, SEG), lambda b, i: (b, 0)),
                    pl.BlockSpec((S, SEG), lambda b, i: (b, C_NV)),
                    ctx_k, ctx_v,
                    pl.BlockSpec((1, NA_HEADS, tq, NA_KROWS * GRID_W), btype)]
        args = (nk_, z, nk_, z, bias)
    else:
        tq, nq = C, 1
        qmap = lambda b, i: (BS // C + b, 0)
        omap = lambda b, i: (b, 0)
        in_specs = [ctx_k, ctx_v]
        args = (nk_, z)
    return pl.pallas_call(
        functools.partial(_na_kernel, has_local=latent, rows=S // GRID_W),
        out_shape=jax.ShapeDtypeStruct((B * nq * tq, SEG), BF16),
        grid=(B, nq),
        in_specs=[pl.BlockSpec((tq, SEG), qmap)] + in_specs,
        out_specs=pl.BlockSpec((tq, SEG), omap),
        compiler_params=_cparams(("parallel", "arbitrary")),
        name="na_lat" if latent else "na_ctx",
    )(nq_, *args)


def _na_bias_tables(rpb, S):
    rows = S // GRID_W
    nqb = rows // NA_QROWS
    i = np.arange(NA_QROWS)[:, None]
    j = np.arange(NA_KROWS)[None, :]
    sel_r = np.zeros((3, NA_QROWS, NA_KROWS, 2 * NA_KR - 1), np.float32)
    for t, qb in enumerate((0, min(1, nqb - 1), nqb - 1)):
        r = qb * NA_QROWS + i
        kr0 = np.clip(qb * NA_QROWS - (NA_KROWS - NA_QROWS) // 2, 0, rows - NA_KROWS)
        kr = kr0 + j
        r0 = np.clip(r - NA_KR // 2, 0, rows - NA_KR)
        ok = (kr >= r0) & (kr < r0 + NA_KR)
        off = np.clip(kr - r + NA_KR - 1, 0, 2 * NA_KR - 2)
        sel_r[t][np.broadcast_to(i, ok.shape)[ok], np.broadcast_to(j, ok.shape)[ok], off[ok]] = 1.0
    cq = np.arange(GRID_W)[:, None]
    ck = np.arange(GRID_W)[None, :]
    c0 = np.clip(cq - NA_KW // 2, 0, GRID_W - NA_KW)
    okc = (ck >= c0) & (ck < c0 + NA_KW)
    offc = np.clip(ck - cq + NA_KW - 1, 0, 2 * NA_KW - 2)
    sel_c = np.zeros((GRID_W, GRID_W, 2 * NA_KW - 1), np.float32)
    sel_c[np.broadcast_to(cq, okc.shape)[okc], np.broadcast_to(ck, okc.shape)[okc], offc[okc]] = 1.0
    valid = sel_r.sum(-1)[:, :, None, :, None] * sel_c.sum(-1)[None, None, :, None, :]
    hp = lax.Precision.HIGHEST
    m1 = jnp.einsum("lhab,qkb->lhaqk", rpb.astype(F32), jnp.asarray(sel_c), precision=hp)
    tab = jnp.einsum("tija,lhaqk->lthiqjk", jnp.asarray(sel_r), m1, precision=hp)
    tab = jnp.where(jnp.asarray(valid)[None, :, None] > 0.5, tab, NEG)
    L, H = rpb.shape[:2]
    return tab.reshape(L, 3, H, NA_QROWS * GRID_W, NA_KROWS * GRID_W).astype(BF16)


def _head_norm(x, width, gain):
    outs = []
    for h in range(x.shape[-1] // width):
        xh = x[:, h * width:(h + 1) * width]
        outs.append(xh * lax.rsqrt(jnp.mean(xh * xh, axis=-1, keepdims=True) + EPS))
    return jnp.concatenate(outs, axis=-1) * gain


def _merge_kernel(x_ref, gate_ref, rf_ref, rb_ref, rg_ref, *refs, diff_scale, n_lat_tiles):
    if n_lat_tiles is None:
        od_ref, on_ref = refs[:2]
        od, on = od_ref[...], on_ref[...]
    else:
        odl_ref, onl_ref, odc_ref, onc_ref = refs[:4]
        is_lat = pl.program_id(0) < n_lat_tiles
        od = jnp.where(is_lat, odl_ref[...], odc_ref[...])
        on = jnp.where(is_lat, onl_ref[...], onc_ref[...])
    g1_ref, g2_ref, g3_ref, retg_ref, dg_ref, wr_ref, wd_ref, wn_ref, wo_ref, o_ref = refs[-10:]
    r = _head_norm(rf_ref[...].astype(F32) + rb_ref[...].astype(F32), RET_DV, retg_ref[...])
    rg = rg_ref[...].astype(F32)
    y_ret = _dot((rg * _sigmoid(rg) * r).astype(BF16), wr_ref[...])
    d = _head_norm(od.astype(F32), DIFF_DV, dg_ref[...]) * diff_scale
    y_diff = _dot(d.astype(BF16), wd_ref[...])
    y_na = _dot(on, wn_ref[...])
    m = (_sigmoid(g1_ref[...].astype(F32)) * y_ret + _sigmoid(g2_ref[...].astype(F32)) * y_diff
         + _sigmoid(g3_ref[...].astype(F32)) * y_na)
    y = _dot(m.astype(BF16), wo_ref[...])
    o_ref[...] = x_ref[...] + gate_ref[0] * y


def _merge(x, gate, rf, rb, z, attn, ret_g, d_g, wr, wd, wn, wo, diff_scale, dims, rows):
    D = x.shape[1]
    tm = _tile(dims["BC"], 256)
    nlt, tpb, B = dims["BS"] // tm, dims["S"] // tm, dims["B"]
    rw = RET_HEADS * RET_DV
    gcol = C_GATES * SEG // D
    row = lambda w, col=0: pl.BlockSpec((tm, w), lambda i: (i, col))
    lat = pl.BlockSpec((tm, SEG), lambda i: (jnp.minimum(i, nlt - 1), 0))
    ctx = pl.BlockSpec((tm, SEG), lambda i: (jnp.maximum(i - nlt, 0), 0))
    const = lambda a: pl.BlockSpec(a.shape, lambda i: (0,) * a.ndim)
    with_ctx = len(attn) == 4
    return pl.pallas_call(
        functools.partial(_merge_kernel, diff_scale=diff_scale, n_lat_tiles=nlt if with_ctx else None),
        out_shape=jax.ShapeDtypeStruct((rows, D), F32),
        grid=(rows // tm,),
        in_specs=[row(D), pl.BlockSpec((1, 1, D), lambda i: (_mod_index(i, nlt, tpb, B), 0, 0)),
                  row(rw), row(rw), row(rw, C_RG * SEG // rw)]
                 + ([lat, lat, ctx, ctx] if with_ctx else [row(SEG), row(SEG)])
                 + [row(D, gcol), row(D, gcol + 1), row(D, gcol + 2),
                    const(ret_g), const(d_g), const(wr), const(wd), const(wn), const(wo)],
        out_specs=row(D),
        compiler_params=_cparams(("parallel",)),
        name="merge",
    )(x, gate, rf, rb, z, *attn, z, z, z, ret_g, d_g, wr, wd, wn, wo)


def _router_kernel(x_ref, g_ref, sh_ref, sc_ref, wrt_ref, br_ref, h_ref, idx_ref, wts_ref, cnt_ref, run_ref):
    i = pl.program_id(0)
    tm = x_ref.shape[0]
    E = N_EXPERTS

    @pl.when(i == 0)
    def _():
        run_ref[...] = jnp.zeros_like(run_ref)

    h = _modnorm(x_ref[...], g_ref[...], sh_ref[0], sc_ref[0])
    h_ref[...] = h
    h_hi = h.astype(BF16)
    h_lo = (h - h_hi.astype(F32)).astype(BF16)
    w = wrt_ref[...]
    w_hi = w.astype(BF16)
    w_lo = (w - w_hi.astype(F32)).astype(BF16)
    logits = _dot_nt(w_hi, h_hi) + _dot_nt(w_hi, h_lo) + _dot_nt(w_lo, h_hi)
    s = _sigmoid(logits)
    sel = s + br_ref[...]

    best, gi = None, None
    for g in range(N_GROUPS):
        r = [sel[g * EXPERTS_PER_GROUP + k:g * EXPERTS_PER_GROUP + k + 1] for k in range(EXPERTS_PER_GROUP)]
        top2 = None
        for a in range(EXPERTS_PER_GROUP):
            for b in range(a + 1, EXPERTS_PER_GROUP):
                pair = r[a] + r[b]
                top2 = pair if top2 is None else jnp.maximum(top2, pair)
        if best is None:
            best, gi = top2, jnp.zeros(top2.shape, I32)
        else:
            upd = top2 > best
            gi = jnp.where(upd, g, gi)
            best = jnp.where(upd, top2, best)

    eid = lax.broadcasted_iota(I32, (E, tm), 0)
    masked = jnp.where(eid // EXPERTS_PER_GROUP == gi, sel, -jnp.inf)
    m1 = masked.max(axis=0, keepdims=True)
    i1 = jnp.where(masked == m1, eid, E).min(axis=0, keepdims=True)
    masked2 = jnp.where(eid == i1, -jnp.inf, masked)
    m2 = masked2.max(axis=0, keepdims=True)
    i2 = jnp.where(masked2 == m2, eid, E).min(axis=0, keepdims=True)
    oh1 = eid == i1
    oh2 = eid == i2
    w1 = jnp.where(oh1, s, 0.0).sum(axis=0, keepdims=True)
    w2 = jnp.where(oh2, s, 0.0).sum(axis=0, keepdims=True)
    tot = w1 + w2

    oh = (oh1 | oh2).astype(F32)
    before = (lax.broadcasted_iota(I32, (tm, tm), 0) < lax.broadcasted_iota(I32, (tm, tm), 1)).astype(BF16)
    prefix = _dot(oh.astype(BF16), before) + run_ref[:, 0:1]
    rank1 = jnp.where(oh1, prefix, 0.0).sum(axis=0, keepdims=True)
    rank2 = jnp.where(oh2, prefix, 0.0).sum(axis=0, keepdims=True)
    run_ref[...] = run_ref[...] + oh.sum(axis=1, keepdims=True)

    idx_ref[...] = jnp.zeros_like(idx_ref)
    wts_ref[...] = jnp.zeros_like(wts_ref)
    for k, v in enumerate((i1, i2, rank1.astype(I32), rank2.astype(I32))):
        idx_ref[0, k:k + 1, :] = v
    wts_ref[0, 0:1, :] = w1 / tot
    wts_ref[0, 1:2, :] = w2 / tot
    cnt_ref[...] = run_ref[...]


def _router(x, g, shift, scale, wrt, br, dims, rows):
    D = x.shape[1]
    tm = _tile(dims["BC"], 256)
    nt = rows // tm
    nlt, tpb, B = dims["BS"] // tm, dims["S"] // tm, dims["B"]
    mod = lambda i: (_mod_index(i, nlt, tpb, B), 0, 0)
    return pl.pallas_call(
        _router_kernel,
        out_shape=[jax.ShapeDtypeStruct((rows, D), F32),
                   jax.ShapeDtypeStruct((nt, 8, tm), I32),
                   jax.ShapeDtypeStruct((nt, 8, tm), F32),
                   jax.ShapeDtypeStruct((N_EXPERTS, 128), F32)],
        grid=(nt,),
        in_specs=[pl.BlockSpec((tm, D), lambda i: (i, 0)),
                  pl.BlockSpec((1, D), lambda i: (0, 0)),
                  pl.BlockSpec((1, 1, D), mod), pl.BlockSpec((1, 1, D), mod),
                  pl.BlockSpec((N_EXPERTS, D), lambda i: (0, 0)),
                  pl.BlockSpec((N_EXPERTS, 1), lambda i: (0, 0))],
        out_specs=[pl.BlockSpec((tm, D), lambda i: (i, 0)),
                   pl.BlockSpec((1, 8, tm), lambda i: (i, 0, 0)),
                   pl.BlockSpec((1, 8, tm), lambda i: (i, 0, 0)),
                   pl.BlockSpec((N_EXPERTS, 128), lambda i: (0, 0))],
        scratch_shapes=[pltpu.VMEM((N_EXPERTS, 128), F32)],
        compiler_params=_cparams(("arbitrary",)),
        name="router",
    )(x, g, shift, scale, wrt, br)


def _gather_rows(n, idx_ref, src_hbm, dst_ref, sem):
    def copy(r):
        return pltpu.make_async_copy(src_hbm.at[pl.ds(idx_ref[0, 0, r], 1)], dst_ref.at[pl.ds(r, 1)], sem)

    def start(r, carry):
        copy(r).start()
        return carry

    def wait(r, carry):
        copy(r).wait()
        return carry

    lax.fori_loop(0, n, start, 0)
    lax.fori_loop(0, n, wait, 0)


def _dispatch_kernel(idx_ref, h_hbm, o_ref, sem):
    _gather_rows(o_ref.shape[0], idx_ref, h_hbm, o_ref, sem)


def _dispatch(src_rows, h, tg):
    P = src_rows.shape[0]
    D = h.shape[1]
    return pl.pallas_call(
        _dispatch_kernel,
        out_shape=jax.ShapeDtypeStruct((P, D), F32),
        grid=(P // tg,),
        in_specs=[pl.BlockSpec((1, 1, tg), lambda i: (i, 0, 0), memory_space=pltpu.SMEM),
                  pl.BlockSpec(memory_space=pl.ANY)],
        out_specs=pl.BlockSpec((tg, D), lambda i: (i, 0)),
        scratch_shapes=[pltpu.SemaphoreType.DMA(())],
        compiler_params=_cparams(("arbitrary",)),
        name="dispatch",
    )(src_rows.reshape(P // tg, 1, tg), h)


def _expert_kernel(te_ref, x_ref, wg_ref, wu_ref, wd_ref, o_ref):
    x = x_ref[...].astype(BF16)
    gate = _dot(x, wg_ref[0])
    a = gate * _sigmoid(gate) * _dot(x, wu_ref[0])
    o_ref[...] = _dot(a.astype(BF16), wd_ref[0])


def _experts(tile_expert, xs, wg, wu, wd, tmE):
    P, D = xs.shape
    De = wg.shape[2]
    return pl.pallas_call(
        _expert_kernel,
        out_shape=jax.ShapeDtypeStruct((P, D), F32),
        grid_spec=pltpu.PrefetchScalarGridSpec(
            num_scalar_prefetch=1,
            grid=(P // tmE,),
            in_specs=[pl.BlockSpec((tmE, D), lambda t, te: (t, 0)),
                      pl.BlockSpec((1, D, De), lambda t, te: (te[t], 0, 0)),
                      pl.BlockSpec((1, D, De), lambda t, te: (te[t], 0, 0)),
                      pl.BlockSpec((1, De, D), lambda t, te: (te[t], 0, 0))],
            out_specs=pl.BlockSpec((tmE, D), lambda t, te: (t, 0))),
        compiler_params=_cparams(("arbitrary",)),
        name="experts",
    )(tile_expert, xs, wg, wu, wd)


def _combine_kernel(p1_ref, p2_ref, x_ref, gate_ref, w_ref, ye_hbm, o_ref, b1_ref, b2_ref, sem1, sem2):
    n = x_ref.shape[0]
    _gather_rows(n, p1_ref, ye_hbm, b1_ref, sem1)
    _gather_rows(n, p2_ref, ye_hbm, b2_ref, sem2)
    w = w_ref[...]
    o_ref[...] = x_ref[...] + gate_ref[0] * (w[:, 0:1] * b1_ref[...] + w[:, 1:2] * b2_ref[...])


def _combine(pos1, pos2, x, gate, w, ye, dims, rows):
    D = x.shape[1]
    tc = _tile(dims["BC"], 256)
    nlt, tpb, B = dims["BS"] // tc, dims["S"] // tc, dims["B"]
    ispec = pl.BlockSpec((1, 1, tc), lambda i: (i, 0, 0), memory_space=pltpu.SMEM)
    return pl.pallas_call(
        _combine_kernel,
        out_shape=jax.ShapeDtypeStruct((rows, D), F32),
        grid=(rows // tc,),
        in_specs=[ispec, ispec,
                  pl.BlockSpec((tc, D), lambda i: (i, 0)),
                  pl.BlockSpec((1, 1, D), lambda i: (_mod_index(i, nlt, tpb, B), 0, 0)),
                  pl.BlockSpec((tc, 2), lambda i: (i, 0)),
                  pl.BlockSpec(memory_space=pl.ANY)],
        out_specs=pl.BlockSpec((tc, D), lambda i: (i, 0)),
        scratch_shapes=[pltpu.VMEM((tc, D), F32), pltpu.VMEM((tc, D), F32),
                        pltpu.SemaphoreType.DMA(()), pltpu.SemaphoreType.DMA(())],
        compiler_params=_cparams(("arbitrary",)),
        name="combine",
    )(pos1.reshape(rows // tc, 1, tc), pos2.reshape(rows // tc, 1, tc), x, gate, w, ye)


def _moe(x, g, shift, scale, gate, wrt, br, wg, wu, wd, dims, rows):
    tmE = 256
    h, idx, wts, cnt = _router(x, g, shift, scale, wrt, br, dims, rows)
    nt = idx.shape[0]
    flat = lambda a, k: a[:, k, :].reshape(rows)
    e1, e2, r1, r2 = (flat(idx, k) for k in range(4))
    counts = cnt[:, 0].astype(I32)
    padded = (counts + tmE - 1) // tmE * tmE
    ends = jnp.cumsum(padded)
    starts = ends - padded
    pos1 = starts[e1] + r1
    pos2 = starts[e2] + r2
    P = (2 * rows + N_EXPERTS * tmE) // tmE * tmE
    tok = jnp.arange(rows, dtype=I32)
    src = jnp.zeros((P,), I32).at[pos1].set(tok).at[pos2].set(tok)
    tile_expert = jnp.minimum(
        jnp.searchsorted(ends, jnp.arange(P // tmE, dtype=I32) * tmE, side="right"), N_EXPERTS - 1).astype(I32)
    xs = _dispatch(src, h, tmE)
    ye = _experts(tile_expert, xs, wg, wu, wd, tmE)
    w = jnp.stack([flat(wts, 0), flat(wts, 1)], axis=-1)
    return _combine(pos1, pos2, x, gate, w, ye, dims, rows)


def kernel(x, c, ctx, c_ctx, w_ada, b_ada, norm1_g, norm2_g, w_in, ret_decay_logit, ret_norm_g, diff_q_norm_g, diff_k_norm_g, diff_lambda, diff_norm_g, na_q_norm_g, na_k_norm_g, na_rpb, w_ret_out, w_diff_out, w_na_out, w_out, w_router, b_router, w_exp_gate, w_exp_up, w_exp_down):
    B, S, D = x.shape
    C = ctx.shape[1]
    L = w_in.shape[0]
    BS, BC = B * S, B * C
    assert S % (NA_QROWS * GRID_W) == 0 and S // GRID_W >= NA_KROWS and C % RET_CHUNK == 0
    assert (C_GATES * SEG) % D == 0 and w_in.shape[2] == C_GATES * SEG + 3 * D
    dims = dict(B=B, S=S, C=C, BS=BS, BC=BC)

    n_c = -(-(B + 1) // 8) * 8
    c_all = jnp.concatenate([c, c_ctx[None], jnp.zeros((n_c - B - 1, D), F32)], axis=0)
    mods = _ada(c_all, w_ada, b_ada)[:, :B + 1]
    mod = lambda l, k: mods[l, :, k * D:(k + 1) * D].reshape(B + 1, 1, D)

    xa = jnp.concatenate([x.reshape(BS, D), ctx.reshape(BC, D)], axis=0)

    tm_prep = _tile(BC, 512)
    tables = _rope_tables(S, tm_prep)
    gmat = jnp.asarray(np.kron(np.eye(SEG // 64), np.full((64, 64), 1.0 / 64)), BF16)
    bias = _na_bias_tables(na_rpb, S)
    lg = jax.nn.log_sigmoid(ret_decay_logit.astype(F32))
    lam_p = diff_lambda.astype(F32)
    wrt = w_router.T
    br = b_router.astype(F32).reshape(N_EXPERTS, 1)
    tile8 = lambda v: jnp.tile(v.astype(F32), SEG // v.shape[0])
    row = lambda v: v.astype(F32).reshape(1, -1)

    for l in range(L):
        last = l == L - 1
        lam_init = 0.8 - 0.6 * math.exp(-0.3 * l)
        lam = (jnp.exp(jnp.sum(lam_p[l, 0] * lam_p[l, 1])) - jnp.exp(jnp.sum(lam_p[l, 2] * lam_p[l, 3]))
               + lam_init).reshape(1, 1)
        z = _inproj(xa, row(norm1_g[l]), mod(l, 0), mod(l, 1), w_in[l].astype(BF16), dims)
        gains = jnp.stack([tile8(diff_q_norm_g[l]), tile8(diff_k_norm_g[l]),
                           tile8(na_q_norm_g[l]), tile8(na_k_norm_g[l])] + [jnp.zeros((SEG,), F32)] * 4)
        rq, rk, dq, dk, nq, nk = _prep(z, gains, gmat, tables, dims)

        rf, rb = _retention(lg[l], rq, rk, z, dims)
        attn = (_diff_attention(lam, dq, dk, z, dims, True), _na_attention(nq, nk, z, bias[l], dims, True))
        rows = BS if last else BS + BC
        if not last:
            attn += (_diff_attention(lam, dq, dk, z, dims, False), _na_attention(nq, nk, z, None, dims, False))
        x1 = _merge(xa, mod(l, 2), rf, rb, z, attn, row(ret_norm_g[l]), row(diff_norm_g[l]),
                    w_ret_out[l].astype(BF16), w_diff_out[l].astype(BF16), w_na_out[l].astype(BF16),
                    w_out[l].astype(BF16), 1.0 - lam_init, dims, rows)
        xa = _moe(x1, row(norm2_g[l]), mod(l, 3), mod(l, 4), mod(l, 5), wrt, br,
                  w_exp_gate[l].astype(BF16), w_exp_up[l].astype(BF16), w_exp_down[l].astype(BF16), dims, rows)
    return xa.reshape(B, S, D)
```

```python
import functools
import math

import numpy as np
import jax
import jax.numpy as jnp
from jax import lax
from jax.experimental import pallas as pl
from jax.experimental.pallas import tpu as pltpu

F32 = jnp.float32
BF16 = jnp.bfloat16
I32 = jnp.int32

GRID_W = 64
RET_HEADS, RET_DK, RET_DV, RET_CHUNK = 4, 128, 256, 128
DIFF_HEADS, DIFF_HD, DIFF_DV = 4, 64, 128
NA_HEADS, NA_HD, NA_KR, NA_KW = 8, 64, 8, 16
N_EXPERTS, N_GROUPS = 16, 4
EXPERTS_PER_GROUP = N_EXPERTS // N_GROUPS
ROPE_BASE = 10000.0
EPS = 1e-6
NEG = -1e30

SEG = 512
C_RQ, C_RK, C_RV, C_RG, C_DQ, C_DK, C_DV, C_NQ, C_NK, C_NV, C_GATES = 0, 1, 2, 4, 6, 7, 8, 9, 10, 11, 12
NA_QROWS = 8
NA_KROWS = 16
KEY_CHUNK = 256
ROW_BLOCK = 128

V7X_VMEM_BYTES = 64 * 1024 * 1024
VMEM_LIMIT = V7X_VMEM_BYTES - 8 * 1024 * 1024


def _cparams(sem):
    return pltpu.CompilerParams(dimension_semantics=sem, vmem_limit_bytes=VMEM_LIMIT)


def _sigmoid(x):
    return 1.0 / (1.0 + jnp.exp(-x))


def _tile(n, pref):
    t = min(n, pref)
    while n % t:
        t //= 2
    return t


def _dot_nt(a, b):
    return lax.dot_general(a, b, (((1,), (1,)), ((), ())), preferred_element_type=F32)


def _dot_tn(a, b):
    return lax.dot_general(a, b, (((0,), (0,)), ((), ())), preferred_element_type=F32)


def _dot(a, b):
    return jnp.dot(a, b, preferred_element_type=F32)


def _ada_kernel(c_ref, w_ref, b_ref, o_ref):
    c = c_ref[...]
    a = (c * _sigmoid(c)).astype(BF16)
    o_ref[0] = _dot(a, w_ref[0].astype(BF16)) + b_ref[0]


def _ada(c_all, w_ada, b_ada):
    L, D, N = w_ada.shape
    M = c_all.shape[0]
    tn = _tile(N, 1024)
    return pl.pallas_call(
        _ada_kernel,
        out_shape=jax.ShapeDtypeStruct((L, M, N), F32),
        grid=(L, N // tn),
        in_specs=[
            pl.BlockSpec((M, D), lambda l, j: (0, 0)),
            pl.BlockSpec((1, D, tn), lambda l, j: (l, 0, j)),
            pl.BlockSpec((1, 1, tn), lambda l, j: (l, 0, j)),
        ],
        out_specs=pl.BlockSpec((1, M, tn), lambda l, j: (l, 0, j)),
        compiler_params=_cparams(("parallel", "parallel")),
        name="ada",
    )(c_all, w_ada, b_ada.reshape(L, 1, N))


def _modnorm(x, g, shift, scale):
    ms = jnp.mean(x * x, axis=-1, keepdims=True)
    y = x * lax.rsqrt(ms + EPS) * g
    return y * (1.0 + scale) + shift


def _inproj_kernel(x_ref, g_ref, sh_ref, sc_ref, w_ref, o_ref, h_ref):
    @pl.when(pl.program_id(1) == 0)
    def _():
        h_ref[...] = _modnorm(x_ref[...], g_ref[...], sh_ref[0], sc_ref[0]).astype(BF16)

    o_ref[...] = _dot(h_ref[...], w_ref[0]).astype(o_ref.dtype)


def _mod_index(i, n_lat_tiles, tiles_per_batch, n_batch):
    return jnp.where(i < n_lat_tiles, i // tiles_per_batch, n_batch)


def _inproj(x, g, shift, scale, w, layer, dims):
    R, D = x.shape
    P = w.shape[2]
    tm = _tile(dims["BC"], 1024)
    tn = _tile(P, 1024)
    nlt, tpb, B = dims["BS"] // tm, dims["S"] // tm, dims["B"]
    mod = lambda i, j: (_mod_index(i, nlt, tpb, B), 0, 0)
    return pl.pallas_call(
        _inproj_kernel,
        out_shape=jax.ShapeDtypeStruct((R, P), BF16),
        grid=(R // tm, P // tn),
        in_specs=[
            pl.BlockSpec((tm, D), lambda i, j: (i, 0)),
            pl.BlockSpec((1, D), lambda i, j: (0, 0)),
            pl.BlockSpec((1, 1, D), mod),
            pl.BlockSpec((1, 1, D), mod),
            pl.BlockSpec((1, D, tn), lambda i, j: (layer, 0, j)),
        ],
        out_specs=pl.BlockSpec((tm, tn), lambda i, j: (i, j)),
        scratch_shapes=[pltpu.VMEM((tm, D), BF16)],
        compiler_params=_cparams(("parallel", "arbitrary")),
        name="inproj",
    )(x, g, shift, scale, w)


def _group_mean_sq(x, gmat):
    xx = x * x
    hi = xx.astype(BF16)
    lo = (xx - hi.astype(F32)).astype(BF16)
    return _dot(hi, gmat) + _dot(lo, gmat)


def _rope(x, cos, sin, half):
    w = x.shape[-1]
    lane = lax.broadcasted_iota(I32, x.shape, 1)
    first = (lane % (2 * half)) < half
    swapped = jnp.where(first, pltpu.roll(x, w - half, 1), pltpu.roll(x, half, 1))
    return x * cos + swapped * sin


def _prep_kernel(rq_ref, rk_ref, dq_ref, dk_ref, nq_ref, nk_ref, gains_ref, gmat_ref, tab_ref,
                 orq_ref, ork_ref, odq_ref, odk_ref, onq_ref, onk_ref):
    reps = SEG // 128
    cos_r = jnp.concatenate([tab_ref[0]] * reps, axis=1)
    sin_r = jnp.concatenate([tab_ref[1]] * reps, axis=1)
    cos_d = jnp.concatenate([tab_ref[2]] * reps, axis=1)
    sin_d = jnp.concatenate([tab_ref[3]] * reps, axis=1)
    gmat = gmat_ref[...]

    def normed(ref, row):
        x = ref[...].astype(F32)
        return x * lax.rsqrt(_group_mean_sq(x, gmat) + EPS) * gains_ref[row:row + 1, :]

    orq_ref[...] = _rope(rq_ref[...].astype(F32), cos_r, sin_r, RET_DK // 4).astype(BF16)
    ork_ref[...] = _rope(rk_ref[...].astype(F32) * (RET_DK ** -0.5), cos_r, sin_r, RET_DK // 4).astype(BF16)
    odq_ref[...] = (_rope(normed(dq_ref, 0), cos_d, sin_d, DIFF_HD // 4) * (DIFF_HD ** -0.5)).astype(BF16)
    odk_ref[...] = _rope(normed(dk_ref, 1), cos_d, sin_d, DIFF_HD // 4).astype(BF16)
    onq_ref[...] = (normed(nq_ref, 2) * (NA_HD ** -0.5)).astype(BF16)
    onk_ref[...] = normed(nk_ref, 3).astype(BF16)


def _prep(z, gains, gmat, tables, dims):
    R = z.shape[0]
    tm = _tile(dims["BC"], 512)
    nlt, tpb = dims["BS"] // tm, dims["S"] // tm
    zspec = lambda c: pl.BlockSpec((tm, SEG), lambda i: (i, c))
    tab = lambda i: (0, jnp.where(i < nlt, i % tpb, tpb), 0)
    ospec = pl.BlockSpec((tm, SEG), lambda i: (i, 0))
    return pl.pallas_call(
        _prep_kernel,
        out_shape=[jax.ShapeDtypeStruct((R, SEG), BF16)] * 6,
        grid=(R // tm,),
        in_specs=[zspec(C_RQ), zspec(C_RK), zspec(C_DQ), zspec(C_DK), zspec(C_NQ), zspec(C_NK),
                  pl.BlockSpec((8, SEG), lambda i: (0, 0)),
                  pl.BlockSpec((SEG, SEG), lambda i: (0, 0)),
                  pl.BlockSpec((4, tm, 128), tab)],
        out_specs=[ospec] * 6,
        compiler_params=_cparams(("parallel",)),
        name="prep",
    )(z, z, z, z, z, z, gains, gmat, tables)


def _rope_tables(S, tm):
    t = jnp.arange(S)
    row, col = t // GRID_W, t % GRID_W

    def one(d, width):
        inv = jnp.power(ROPE_BASE, -jnp.arange(0, d, 2, dtype=F32) / d)
        ar = row.astype(F32)[:, None] * inv[None, :]
        ac = col.astype(F32)[:, None] * inv[None, :]
        cos = jnp.concatenate([jnp.cos(ar), jnp.cos(ar), jnp.cos(ac), jnp.cos(ac)], axis=-1)
        sin = jnp.concatenate([-jnp.sin(ar), jnp.sin(ar), -jnp.sin(ac), jnp.sin(ac)], axis=-1)
        reps = width // cos.shape[-1]
        return jnp.tile(cos, (1, reps)), jnp.tile(sin, (1, reps))

    cr, sr = one(RET_DK // 2, 128)
    cd, sd = one(DIFF_HD // 2, 128)
    tab = jnp.stack([cr, sr, cd, sd])
    ident = jnp.stack([jnp.ones((tm, 128), F32), jnp.zeros((tm, 128), F32)] * 2)
    return jnp.concatenate([tab, ident], axis=1)


def _ret_kernel(lg_ref, qf_ref, kf_ref, vf_ref, qb_ref, kb_ref, vb_ref, of_ref, ob_ref, s_ref):
    C = RET_CHUNK

    @pl.when(pl.program_id(1) == 0)
    def _():
        s_ref[...] = jnp.zeros_like(s_ref)

    ii = lax.broadcasted_iota(I32, (C, C), 0).astype(F32)
    jj = lax.broadcasted_iota(I32, (C, C), 1).astype(F32)
    ri = lax.broadcasted_iota(I32, (C, RET_DK), 0).astype(F32)

    for h in range(RET_HEADS):
        qs = slice(h * RET_DK, (h + 1) * RET_DK)
        vs = slice(h * RET_DV, (h + 1) * RET_DV)
        for d, (q_ref, k_ref, v_ref, o_ref) in enumerate(((qf_ref, kf_ref, vf_ref, of_ref),
                                                         (qb_ref, kb_ref, vb_ref, ob_ref))):
            lg = lg_ref[d, h]
            q, k, v = q_ref[:, qs], k_ref[:, qs], v_ref[:, vs]
            if d == 0:
                dist = ii - jj
                q_dec = jnp.exp(lg * (ri + 1.0))
                k_dec = jnp.exp(lg * (C - 1.0 - ri))
            else:
                dist = jj - ii
                q_dec = jnp.exp(lg * (C - ri))
                k_dec = jnp.exp(lg * ri)
            decay = jnp.where(dist >= 0, jnp.exp(lg * jnp.maximum(dist, 0.0)), 0.0)
            att = (_dot_nt(q, k) * decay).astype(BF16)
            s = s_ref[d, h]
            o = _dot(att, v) + _dot((q.astype(F32) * q_dec).astype(BF16), s.astype(BF16))
            s_ref[d, h] = s * jnp.exp(lg * C) + _dot_tn((k.astype(F32) * k_dec).astype(BF16), v)
            o_ref[:, vs] = o.astype(o_ref.dtype)


def _retention(lg, rq, rk, z, dims):
    B, S, C, BS = dims["B"], dims["S"], dims["C"], dims["BS"]
    R = rq.shape[0]
    nc, nl = C // RET_CHUNK, S // RET_CHUNK
    base = BS // RET_CHUNK
    fwd = lambda b, c: jnp.where(c < nc, base + b * nc + c, b * nl + (c - nc))
    bwd = lambda b, c: jnp.where(c < nc, base + b * nc + (nc - 1 - c), b * nl + (nl - 1 - (c - nc)))
    vw = RET_HEADS * RET_DV
    spec = lambda w, col, f: pl.BlockSpec((RET_CHUNK, w), lambda b, c: (f(b, c), col))
    return pl.pallas_call(
        _ret_kernel,
        out_shape=[jax.ShapeDtypeStruct((R, vw), BF16), jax.ShapeDtypeStruct((R, vw), BF16)],
        grid=(B, nc + nl),
        in_specs=[pl.BlockSpec(memory_space=pltpu.SMEM),
                  spec(SEG, 0, fwd), spec(SEG, 0, fwd), spec(vw, C_RV * SEG // vw, fwd),
                  spec(SEG, 0, bwd), spec(SEG, 0, bwd), spec(vw, C_RV * SEG // vw, bwd)],
        out_specs=[spec(vw, 0, fwd), spec(vw, 0, bwd)],
        scratch_shapes=[pltpu.VMEM((2, RET_HEADS, RET_DK, RET_DV), F32)],
        compiler_params=_cparams(("parallel", "arbitrary")),
        name="retention",
    )(lg, rq, rk, z, rq, rk, z)


def _split_halves(q):
    lane = lax.broadcasted_iota(I32, q.shape, 1)
    lo = jnp.where(lane < 64, q, jnp.zeros_like(q))
    hi = jnp.where(lane >= 64, q, jnp.zeros_like(q))
    return jnp.concatenate([lo, hi], axis=0)


def _fold_lanes(x, op):
    acc = x[:, :128]
    for j in range(1, x.shape[1] // 128):
        acc = op(acc, x[:, j * 128:(j + 1) * 128])
    return acc


def _attend(q2, sources, lanes, s_ref, e_ref):
    chunks = []
    off = 0
    for k_ref, v_ref, row0, n_rows, bias_fn in sources:
        for c0 in range(0, n_rows, KEY_CHUNK):
            n = min(KEY_CHUNK, n_rows - c0)
            chunks.append((k_ref, v_ref, row0 + c0, n, off, c0, bias_fn))
            off += n
    aligned = lambda r: r if isinstance(r, int) else pl.multiple_of(r, 128)
    accs, sums = [], []
    for b, rb in enumerate(range(0, q2.shape[0], ROW_BLOCK)):
        q = q2[rb:rb + ROW_BLOCK]
        buf = b % 2
        m = None
        for k_ref, _, r0, n, off, c0, bias_fn in chunks:
            s = _dot_nt(q, k_ref[pl.ds(aligned(r0), n), lanes])
            if bias_fn is not None:
                s = s + bias_fn(rb, c0, n)
            s_ref[buf, :, off:off + n] = s
            blk = _fold_lanes(s, jnp.maximum)
            m = blk if m is None else jnp.maximum(m, blk)
        m = m.max(axis=-1, keepdims=True)
        l = jnp.zeros((ROW_BLOCK, 128), F32)
        for _, _, _, n, off, _, _ in chunks:
            e = jnp.exp(s_ref[buf, :, off:off + n] - m)
            l = l + _fold_lanes(e, jnp.add)
            e_ref[buf, :, off:off + n] = e.astype(BF16)
        acc, off = None, 0
        for _, v_ref, row0, n_rows, _ in sources:
            t = _dot(e_ref[buf, :, off:off + n_rows], v_ref[pl.ds(aligned(row0), n_rows), lanes])
            acc = t if acc is None else acc + t
            off += n_rows
        accs.append(acc)
        sums.append(l.sum(axis=-1, keepdims=True))
    return jnp.concatenate(accs, axis=0), jnp.concatenate(sums, axis=0)


def _diff_kernel(lam_ref, q_ref, *refs, has_latent):
    if has_latent:
        kl_ref, vl_ref, kc_ref, vc_ref, o_ref, s_ref, e_ref = refs
        sources = ((kl_ref, vl_ref, 0, kl_ref.shape[0], None), (kc_ref, vc_ref, 0, kc_ref.shape[0], None))
    else:
        kc_ref, vc_ref, o_ref, s_ref, e_ref = refs
        sources = ((kc_ref, vc_ref, 0, kc_ref.shape[0], None),)
    lam = lam_ref[0, 0]
    tq = q_ref.shape[0]
    for h in range(DIFF_HEADS):
        hs = slice(h * 128, (h + 1) * 128)
        acc, l = _attend(_split_halves(q_ref[:, hs]), sources, hs, s_ref, e_ref)
        o = acc[:tq] * (1.0 / l[:tq]) - acc[tq:] * (lam / l[tq:])
        o_ref[:, hs] = o.astype(o_ref.dtype)


def _diff_attention(lam, dq, dk, z, dims, latent):
    B, S, C, BS = dims["B"], dims["S"], dims["C"], dims["BS"]
    lat_k = pl.BlockSpec((S, SEG), lambda b, i: (b, 0))
    lat_v = pl.BlockSpec((S, SEG), lambda b, i: (b, C_DV))
    ctx_k = pl.BlockSpec((C, SEG), lambda b, i: (BS // C + b, 0))
    ctx_v = pl.BlockSpec((C, SEG), lambda b, i: (BS // C + b, C_DV))
    if latent:
        tq = _tile(S, 256)
        nq = S // tq
        qmap = omap = lambda b, i: (b * nq + i, 0)
        in_specs = [lat_k, lat_v, ctx_k, ctx_v]
        args = (dk, z, dk, z)
        n_keys = S + C
    else:
        tq, nq = C, 1
        qmap = lambda b, i: (BS // C + b, 0)
        omap = lambda b, i: (b, 0)
        in_specs = [ctx_k, ctx_v]
        args = (dk, z)
        n_keys = C
    return pl.pallas_call(
        functools.partial(_diff_kernel, has_latent=latent),
        out_shape=jax.ShapeDtypeStruct((B * nq * tq, SEG), BF16),
        grid=(B, nq),
        in_specs=[pl.BlockSpec(memory_space=pltpu.SMEM), pl.BlockSpec((tq, SEG), qmap)] + in_specs,
        out_specs=pl.BlockSpec((tq, SEG), omap),
        scratch_shapes=[pltpu.VMEM((2, ROW_BLOCK, n_keys), F32), pltpu.VMEM((2, ROW_BLOCK, n_keys), BF16)],
        compiler_params=_cparams(("parallel", "arbitrary")),
        name="diff_lat" if latent else "diff_ctx",
    )(lam, dq, *args)


def _na_kernel(q_ref, *refs, has_local, rows):
    if has_local:
        kl_ref, vl_ref, kc_ref, vc_ref, bias_ref, o_ref = refs
        qb = pl.program_id(1)
        kr0 = jnp.clip(qb * NA_QROWS - (NA_KROWS - NA_QROWS) // 2, 0, rows - NA_KROWS)
        k0 = pl.multiple_of(kr0 * GRID_W, 256)
        nk = NA_KROWS * GRID_W
    else:
        kc_ref, vc_ref, o_ref = refs
    tq = q_ref.shape[0]
    lane = lax.broadcasted_iota(I32, (tq, 128), 1)
    for p in range(NA_HEADS // 2):
        ps = slice(p * 128, (p + 1) * 128)
        q2 = _split_halves(q_ref[:, ps])
        parts = [_dot_nt(q2, kc_ref[:, ps])]
        vals = [vc_ref[:, ps]]
        if has_local:
            s_loc = _dot_nt(q2, kl_ref[pl.ds(k0, nk), ps])
            bias = jnp.concatenate([bias_ref[0, 2 * p], bias_ref[0, 2 * p + 1]], axis=0).astype(F32)
            parts.append(s_loc + bias)
            vals.append(vl_ref[pl.ds(k0, nk), ps])
        m = parts[0].max(axis=-1, keepdims=True)
        for s in parts[1:]:
            m = jnp.maximum(m, s.max(axis=-1, keepdims=True))
        es = [jnp.exp(s - m) for s in parts]
        den = es[0].sum(axis=-1, keepdims=True)
        for e in es[1:]:
            den = den + e.sum(axis=-1, keepdims=True)
        rden = 1.0 / den
        o = None
        for e, v in zip(es, vals):
            t = _dot((e * rden).astype(BF16), v)
            o = t if o is None else o + t
        o_ref[:, ps] = jnp.where(lane < 64, o[:tq], o[tq:]).astype(o_ref.dtype)


def _na_attention(nq_, nk_, z, bias, layer, dims, latent):
    B, S, C, BS = dims["B"], dims["S"], dims["C"], dims["BS"]
    ctx_k = pl.BlockSpec((C, SEG), lambda b, i: (BS // C + b, 0))
    ctx_v = pl.BlockSpec((C, SEG), lambda b, i: (BS // C + b, C_NV))
    if latent:
        tq = NA_QROWS * GRID_W
        nq = S // tq
        qmap = omap = lambda b, i: (b * nq + i, 0)
        btype = lambda b, i: (3 * layer + jnp.where(i == 0, 0, jnp.where(i == nq - 1, 2, 1)), 0, 0, 0)
        in_specs = [pl.BlockSpec((S, SEG), lambda b, i: (b, 0)),
                    pl.BlockSpec((S, SEG), lambda b, i: (b, C_NV)),
                    ctx_k, ctx_v,
                    pl.BlockSpec((1, NA_HEADS, tq, NA_KROWS * GRID_W), btype)]
        args = (nk_, z, nk_, z, bias)
    else:
        tq, nq = C, 1
        qmap = lambda b, i: (BS // C + b, 0)
        omap = lambda b, i: (b, 0)
        in_specs = [ctx_k, ctx_v]
        args = (nk_, z)
    return pl.pallas_call(
        functools.partial(_na_kernel, has_local=latent, rows=S // GRID_W),
        out_shape=jax.ShapeDtypeStruct((B * nq * tq, SEG), BF16),
        grid=(B, nq),
        in_specs=[pl.BlockSpec((tq, SEG), qmap)] + in_specs,
        out_specs=pl.BlockSpec((tq, SEG), omap),
        compiler_params=_cparams(("parallel", "arbitrary")),
        name="na_lat" if latent else "na_ctx",
    )(nq_, *args)


def _na_bias_tables(rpb, S):
    rows = S // GRID_W
    nqb = rows // NA_QROWS
    i = np.arange(NA_QROWS)[:, None]
    j = np.arange(NA_KROWS)[None, :]
    sel_r = np.zeros((3, NA_QROWS, NA_KROWS, 2 * NA_KR - 1), np.float32)
    for t, qb in enumerate((0, min(1, nqb - 1), nqb - 1)):
        r = qb * NA_QROWS + i
        kr0 = np.clip(qb * NA_QROWS - (NA_KROWS - NA_QROWS) // 2, 0, rows - NA_KROWS)
        kr = kr0 + j
        r0 = np.clip(r - NA_KR // 2, 0, rows - NA_KR)
        ok = (kr >= r0) & (kr < r0 + NA_KR)
        off = np.clip(kr - r + NA_KR - 1, 0, 2 * NA_KR - 2)
        sel_r[t][np.broadcast_to(i, ok.shape)[ok], np.broadcast_to(j, ok.shape)[ok], off[ok]] = 1.0
    cq = np.arange(GRID_W)[:, None]
    ck = np.arange(GRID_W)[None, :]
    c0 = np.clip(cq - NA_KW // 2, 0, GRID_W - NA_KW)
    okc = (ck >= c0) & (ck < c0 + NA_KW)
    offc = np.clip(ck - cq + NA_KW - 1, 0, 2 * NA_KW - 2)
    sel_c = np.zeros((GRID_W, GRID_W, 2 * NA_KW - 1), np.float32)
    sel_c[np.broadcast_to(cq, okc.shape)[okc], np.broadcast_to(ck, okc.shape)[okc], offc[okc]] = 1.0
    valid = sel_r.sum(-1)[:, :, None, :, None] * sel_c.sum(-1)[None, None, :, None, :]
    hp = lax.Precision.HIGHEST
    m1 = jnp.einsum("lhab,qkb->lhaqk", rpb.astype(F32), jnp.asarray(sel_c), precision=hp)
    tab = jnp.einsum("tija,lhaqk->lthiqjk", jnp.asarray(sel_r), m1, precision=hp)
    tab = jnp.where(jnp.asarray(valid)[None, :, None] > 0.5, tab, NEG)
    L, H = rpb.shape[:2]
    return tab.reshape(L * 3, H, NA_QROWS * GRID_W, NA_KROWS * GRID_W).astype(BF16)


def _head_norm(x, width, gain):
    outs = []
    for h in range(x.shape[-1] // width):
        xh = x[:, h * width:(h + 1) * width]
        outs.append(xh * lax.rsqrt(jnp.mean(xh * xh, axis=-1, keepdims=True) + EPS))
    return jnp.concatenate(outs, axis=-1) * gain


def _merge_kernel(x_ref, gate_ref, rf_ref, rb_ref, rg_ref, *refs, diff_scale, n_lat_tiles):
    if n_lat_tiles is None:
        od_ref, on_ref = refs[:2]
        od, on = od_ref[...], on_ref[...]
    else:
        odl_ref, onl_ref, odc_ref, onc_ref = refs[:4]
        is_lat = pl.program_id(0) < n_lat_tiles
        od = jnp.where(is_lat, odl_ref[...], odc_ref[...])
        on = jnp.where(is_lat, onl_ref[...], onc_ref[...])
    g1_ref, g2_ref, g3_ref, retg_ref, dg_ref, wr_ref, wd_ref, wn_ref, wo_ref, o_ref = refs[-10:]
    r = _head_norm(rf_ref[...].astype(F32) + rb_ref[...].astype(F32), RET_DV, retg_ref[...])
    rg = rg_ref[...].astype(F32)
    y_ret = _dot((rg * _sigmoid(rg) * r).astype(BF16), wr_ref[0])
    d = _head_norm(od.astype(F32), DIFF_DV, dg_ref[...]) * diff_scale
    y_diff = _dot(d.astype(BF16), wd_ref[0])
    y_na = _dot(on, wn_ref[0])
    m = (_sigmoid(g1_ref[...].astype(F32)) * y_ret + _sigmoid(g2_ref[...].astype(F32)) * y_diff
         + _sigmoid(g3_ref[...].astype(F32)) * y_na)
    y = _dot(m.astype(BF16), wo_ref[0])
    o_ref[...] = x_ref[...] + gate_ref[0] * y


def _merge(x, gate, rf, rb, z, attn, ret_g, d_g, wr, wd, wn, wo, layer, diff_scale, dims, rows):
    D = x.shape[1]
    tm = _tile(dims["BC"], 256)
    nlt, tpb, B = dims["BS"] // tm, dims["S"] // tm, dims["B"]
    rw = RET_HEADS * RET_DV
    gcol = C_GATES * SEG // D
    row = lambda w, col=0: pl.BlockSpec((tm, w), lambda i: (i, col))
    lat = pl.BlockSpec((tm, SEG), lambda i: (jnp.minimum(i, nlt - 1), 0))
    ctx = pl.BlockSpec((tm, SEG), lambda i: (jnp.maximum(i - nlt, 0), 0))
    const = lambda a: pl.BlockSpec(a.shape, lambda i: (0,) * a.ndim)
    wspec = lambda a: pl.BlockSpec((1,) + a.shape[1:], lambda i: (layer, 0, 0))
    with_ctx = len(attn) == 4
    return pl.pallas_call(
        functools.partial(_merge_kernel, diff_scale=diff_scale, n_lat_tiles=nlt if with_ctx else None),
        out_shape=jax.ShapeDtypeStruct((rows, D), F32),
        grid=(rows // tm,),
        in_specs=[row(D), pl.BlockSpec((1, 1, D), lambda i: (_mod_index(i, nlt, tpb, B), 0, 0)),
                  row(rw), row(rw), row(rw, C_RG * SEG // rw)]
                 + ([lat, lat, ctx, ctx] if with_ctx else [row(SEG), row(SEG)])
                 + [row(D, gcol), row(D, gcol + 1), row(D, gcol + 2),
                    const(ret_g), const(d_g), wspec(wr), wspec(wd), wspec(wn), wspec(wo)],
        out_specs=row(D),
        compiler_params=_cparams(("parallel",)),
        name="merge",
    )(x, gate, rf, rb, z, *attn, z, z, z, ret_g, d_g, wr, wd, wn, wo)


def _to_token_tiles(ref, x):
    rows, width = x.shape
    n = width // 128
    for c in range(n):
        ref[pl.ds(c, rows, stride=n), :] = x[:, c * 128:(c + 1) * 128]


def _from_token_tiles(ref, rows):
    n = ref.shape[0] // rows
    return jnp.concatenate([ref[pl.ds(c, rows, stride=n), :] for c in range(n)], axis=1)


def _router_kernel(x_ref, g_ref, sh_ref, sc_ref, wrt_ref, br_ref, h_ref, idx_ref, wts_ref, cnt_ref, run_ref):
    i = pl.program_id(0)
    tm = x_ref.shape[0]
    E = N_EXPERTS

    @pl.when(i == 0)
    def _():
        run_ref[...] = jnp.zeros_like(run_ref)

    h = _modnorm(x_ref[...], g_ref[...], sh_ref[0], sc_ref[0])
    _to_token_tiles(h_ref, h)
    h_hi = h.astype(BF16)
    h_lo = (h - h_hi.astype(F32)).astype(BF16)
    w = wrt_ref[...]
    w_hi = w.astype(BF16)
    w_lo = (w - w_hi.astype(F32)).astype(BF16)
    logits = _dot_nt(w_hi, h_hi) + _dot_nt(w_hi, h_lo) + _dot_nt(w_lo, h_hi)
    s = _sigmoid(logits)
    sel = s + br_ref[...]

    best, gi = None, None
    for g in range(N_GROUPS):
        r = [sel[g * EXPERTS_PER_GROUP + k:g * EXPERTS_PER_GROUP + k + 1] for k in range(EXPERTS_PER_GROUP)]
        top2 = None
        for a in range(EXPERTS_PER_GROUP):
            for b in range(a + 1, EXPERTS_PER_GROUP):
                pair = r[a] + r[b]
                top2 = pair if top2 is None else jnp.maximum(top2, pair)
        if best is None:
            best, gi = top2, jnp.zeros(top2.shape, I32)
        else:
            upd = top2 > best
            gi = jnp.where(upd, g, gi)
            best = jnp.where(upd, top2, best)

    eid = lax.broadcasted_iota(I32, (E, tm), 0)
    masked = jnp.where(eid // EXPERTS_PER_GROUP == gi, sel, -jnp.inf)
    m1 = masked.max(axis=0, keepdims=True)
    i1 = jnp.where(masked == m1, eid, E).min(axis=0, keepdims=True)
    masked2 = jnp.where(eid == i1, -jnp.inf, masked)
    m2 = masked2.max(axis=0, keepdims=True)
    i2 = jnp.where(masked2 == m2, eid, E).min(axis=0, keepdims=True)
    oh1 = eid == i1
    oh2 = eid == i2
    w1 = jnp.where(oh1, s, 0.0).sum(axis=0, keepdims=True)
    w2 = jnp.where(oh2, s, 0.0).sum(axis=0, keepdims=True)
    tot = w1 + w2

    oh = (oh1 | oh2).astype(F32)
    before = (lax.broadcasted_iota(I32, (tm, tm), 0) < lax.broadcasted_iota(I32, (tm, tm), 1)).astype(BF16)
    prefix = _dot(oh.astype(BF16), before) + run_ref[:, 0:1]
    rank1 = jnp.where(oh1, prefix, 0.0).sum(axis=0, keepdims=True)
    rank2 = jnp.where(oh2, prefix, 0.0).sum(axis=0, keepdims=True)
    run_ref[...] = run_ref[...] + oh.sum(axis=1, keepdims=True)

    idx_ref[...] = jnp.zeros_like(idx_ref)
    wts_ref[...] = jnp.zeros_like(wts_ref)
    for k, v in enumerate((i1, i2, rank1.astype(I32), rank2.astype(I32))):
        idx_ref[0, k:k + 1, :] = v
    wts_ref[0, 0:1, :] = w1 / tot
    wts_ref[0, 1:2, :] = w2 / tot
    cnt_ref[...] = run_ref[...]


def _router(x, g, shift, scale, wrt, br, dims, rows):
    D = x.shape[1]
    tm = _tile(dims["BC"], 256)
    nt = rows // tm
    nlt, tpb, B = dims["BS"] // tm, dims["S"] // tm, dims["B"]
    mod = lambda i: (_mod_index(i, nlt, tpb, B), 0, 0)
    n = D // 128
    return pl.pallas_call(
        _router_kernel,
        out_shape=[jax.ShapeDtypeStruct((rows * n, 128), F32),
                   jax.ShapeDtypeStruct((nt, 8, tm), I32),
                   jax.ShapeDtypeStruct((nt, 8, tm), F32),
                   jax.ShapeDtypeStruct((N_EXPERTS, 128), F32)],
        grid=(nt,),
        in_specs=[pl.BlockSpec((tm, D), lambda i: (i, 0)),
                  pl.BlockSpec((1, D), lambda i: (0, 0)),
                  pl.BlockSpec((1, 1, D), mod), pl.BlockSpec((1, 1, D), mod),
                  pl.BlockSpec((N_EXPERTS, D), lambda i: (0, 0)),
                  pl.BlockSpec((N_EXPERTS, 1), lambda i: (0, 0))],
        out_specs=[pl.BlockSpec((tm * n, 128), lambda i: (i, 0)),
                   pl.BlockSpec((1, 8, tm), lambda i: (i, 0, 0)),
                   pl.BlockSpec((1, 8, tm), lambda i: (i, 0, 0)),
                   pl.BlockSpec((N_EXPERTS, 128), lambda i: (0, 0))],
        scratch_shapes=[pltpu.VMEM((N_EXPERTS, 128), F32)],
        compiler_params=_cparams(("arbitrary",)),
        name="router",
    )(x, g, shift, scale, wrt, br)


GATHER_UNROLL = 8


def _gather_tokens(n_tok, n, idx_ref, src_hbm, dst_ref, sem):
    def copy(r):
        src = pl.multiple_of(idx_ref[0, 0, r] * n, n)
        dst = pl.multiple_of(r * n, n)
        return pltpu.make_async_copy(src_hbm.at[pl.ds(src, n)], dst_ref.at[pl.ds(dst, n)], sem)

    def start(g, carry):
        for u in range(GATHER_UNROLL):
            copy(g * GATHER_UNROLL + u).start(priority=u % 2)
        return carry

    def wait(g, carry):
        for u in range(GATHER_UNROLL):
            copy(g * GATHER_UNROLL + u).wait()
        return carry

    lax.fori_loop(0, n_tok // GATHER_UNROLL, start, 0)
    lax.fori_loop(0, n_tok // GATHER_UNROLL, wait, 0)


def _dispatch_kernel(idx_ref, h_hbm, o_ref, sem, *, n):
    _gather_tokens(o_ref.shape[0] // n, n, idx_ref, h_hbm, o_ref, sem)


def _dispatch(src_rows, h, tg, n):
    P = src_rows.shape[0]
    return pl.pallas_call(
        functools.partial(_dispatch_kernel, n=n),
        out_shape=jax.ShapeDtypeStruct((P * n, 128), F32),
        grid=(P // tg,),
        in_specs=[pl.BlockSpec((1, 1, tg), lambda i: (i, 0, 0), memory_space=pltpu.SMEM),
                  pl.BlockSpec(memory_space=pl.ANY)],
        out_specs=pl.BlockSpec((tg * n, 128), lambda i: (i, 0)),
        scratch_shapes=[pltpu.SemaphoreType.DMA(())],
        compiler_params=_cparams(("arbitrary",)),
        name="dispatch",
    )(src_rows.reshape(P // tg, 1, tg), h)


def _expert_kernel(te_ref, x_ref, wg_ref, wu_ref, wd_ref, o_ref, *, rows):
    x = _from_token_tiles(x_ref, rows).astype(BF16)
    gate = _dot(x, wg_ref[0, 0])
    a = gate * _sigmoid(gate) * _dot(x, wu_ref[0, 0])
    _to_token_tiles(o_ref, _dot(a.astype(BF16), wd_ref[0, 0]))


def _experts(tile_expert, xs, wg, wu, wd, layer, tmE):
    D, De = wg.shape[2:]
    n = D // 128
    P = xs.shape[0] // n
    wspec = lambda a, b: pl.BlockSpec((1, 1, a, b), lambda t, te: (layer, te[t], 0, 0))
    return pl.pallas_call(
        functools.partial(_expert_kernel, rows=tmE),
        out_shape=jax.ShapeDtypeStruct((P * n, 128), F32),
        grid_spec=pltpu.PrefetchScalarGridSpec(
            num_scalar_prefetch=1,
            grid=(P // tmE,),
            in_specs=[pl.BlockSpec((tmE * n, 128), lambda t, te: (t, 0)),
                      wspec(D, De), wspec(D, De), wspec(De, D)],
            out_specs=pl.BlockSpec((tmE * n, 128), lambda t, te: (t, 0))),
        compiler_params=_cparams(("arbitrary",)),
        name="experts",
    )(tile_expert, xs, wg, wu, wd)


def _combine_kernel(p1_ref, p2_ref, x_ref, gate_ref, w_ref, ye_hbm, o_ref, b1_ref, b2_ref, sem1, sem2):
    rows, width = x_ref.shape
    n = width // 128
    _gather_tokens(rows, n, p1_ref, ye_hbm, b1_ref, sem1)
    _gather_tokens(rows, n, p2_ref, ye_hbm, b2_ref, sem2)
    w = w_ref[...]
    y = w[:, 0:1] * _from_token_tiles(b1_ref, rows) + w[:, 1:2] * _from_token_tiles(b2_ref, rows)
    o_ref[...] = x_ref[...] + gate_ref[0] * y


def _combine(pos1, pos2, x, gate, w, ye, dims, rows):
    D = x.shape[1]
    n = D // 128
    tc = _tile(dims["BC"], 256)
    nlt, tpb, B = dims["BS"] // tc, dims["S"] // tc, dims["B"]
    ispec = pl.BlockSpec((1, 1, tc), lambda i: (i, 0, 0), memory_space=pltpu.SMEM)
    return pl.pallas_call(
        _combine_kernel,
        out_shape=jax.ShapeDtypeStruct((rows, D), F32),
        grid=(rows // tc,),
        in_specs=[ispec, ispec,
                  pl.BlockSpec((tc, D), lambda i: (i, 0)),
                  pl.BlockSpec((1, 1, D), lambda i: (_mod_index(i, nlt, tpb, B), 0, 0)),
                  pl.BlockSpec((tc, 2), lambda i: (i, 0)),
                  pl.BlockSpec(memory_space=pl.ANY)],
        out_specs=pl.BlockSpec((tc, D), lambda i: (i, 0)),
        scratch_shapes=[pltpu.VMEM((tc * n, 128), F32), pltpu.VMEM((tc * n, 128), F32),
                        pltpu.SemaphoreType.DMA(()), pltpu.SemaphoreType.DMA(())],
        compiler_params=_cparams(("arbitrary",)),
        name="combine",
    )(pos1.reshape(rows // tc, 1, tc), pos2.reshape(rows // tc, 1, tc), x, gate, w, ye)


def _moe(x, g, shift, scale, gate, wrt, br, wg, wu, wd, layer, dims, rows):
    tmE = 256
    n = x.shape[1] // 128
    h, idx, wts, cnt = _router(x, g, shift, scale, wrt, br, dims, rows)
    nt = idx.shape[0]
    flat = lambda a, k: a[:, k, :].reshape(rows)
    e1, e2, r1, r2 = (flat(idx, k) for k in range(4))
    counts = cnt[:, 0].astype(I32)
    padded = (counts + tmE - 1) // tmE * tmE
    ends = jnp.cumsum(padded)
    starts = ends - padded
    pos1 = starts[e1] + r1
    pos2 = starts[e2] + r2
    P = (2 * rows + N_EXPERTS * tmE) // tmE * tmE
    tok = jnp.arange(rows, dtype=I32)
    src = jnp.zeros((P,), I32).at[jnp.concatenate([pos1, pos2])].set(
        jnp.concatenate([tok, tok]), unique_indices=True, mode="promise_in_bounds")
    tile_start = jnp.arange(P // tmE, dtype=I32) * tmE
    tile_expert = jnp.minimum(jnp.sum(ends[None, :] <= tile_start[:, None], axis=1), N_EXPERTS - 1).astype(I32)
    xs = _dispatch(src, h, tmE, n)
    ye = _experts(tile_expert, xs, wg, wu, wd, layer, tmE)
    w = jnp.stack([flat(wts, 0), flat(wts, 1)], axis=-1)
    return _combine(pos1, pos2, x, gate, w, ye, dims, rows)


def kernel(x, c, ctx, c_ctx, w_ada, b_ada, norm1_g, norm2_g, w_in, ret_decay_logit, ret_norm_g, diff_q_norm_g, diff_k_norm_g, diff_lambda, diff_norm_g, na_q_norm_g, na_k_norm_g, na_rpb, w_ret_out, w_diff_out, w_na_out, w_out, w_router, b_router, w_exp_gate, w_exp_up, w_exp_down):
    B, S, D = x.shape
    C = ctx.shape[1]
    L = w_in.shape[0]
    BS, BC = B * S, B * C
    assert S % (NA_QROWS * GRID_W) == 0 and S // GRID_W >= NA_KROWS and C % RET_CHUNK == 0
    assert (C_GATES * SEG) % D == 0 and w_in.shape[2] == C_GATES * SEG + 3 * D
    dims = dict(B=B, S=S, C=C, BS=BS, BC=BC)

    n_c = -(-(B + 1) // 8) * 8
    c_all = jnp.concatenate([c, c_ctx[None], jnp.zeros((n_c - B - 1, D), F32)], axis=0)
    mods = _ada(c_all, w_ada, b_ada)[:, :B + 1]
    mod = lambda l, k: mods[l, :, k * D:(k + 1) * D].reshape(B + 1, 1, D)

    xa = jnp.concatenate([x.reshape(BS, D), ctx.reshape(BC, D)], axis=0)

    tm_prep = _tile(BC, 512)
    tables = _rope_tables(S, tm_prep)
    gmat = jnp.asarray(np.kron(np.eye(SEG // 64), np.full((64, 64), 1.0 / 64)), BF16)
    bias = _na_bias_tables(na_rpb, S)
    lg = jax.nn.log_sigmoid(ret_decay_logit.astype(F32))
    lam_p = diff_lambda.astype(F32)
    wrt = w_router.T
    br = b_router.astype(F32).reshape(N_EXPERTS, 1)
    tile8 = lambda v: jnp.tile(v.astype(F32), SEG // v.shape[0])
    row = lambda v: v.astype(F32).reshape(1, -1)
    w_in_b, w_ret_b, w_diff_b, w_na_b, w_out_b, w_gate_b, w_up_b, w_down_b = (
        w.astype(BF16) for w in (w_in, w_ret_out, w_diff_out, w_na_out, w_out, w_exp_gate, w_exp_up, w_exp_down))

    for l in range(L):
        last = l == L - 1
        lam_init = 0.8 - 0.6 * math.exp(-0.3 * l)
        lam = (jnp.exp(jnp.sum(lam_p[l, 0] * lam_p[l, 1])) - jnp.exp(jnp.sum(lam_p[l, 2] * lam_p[l, 3]))
               + lam_init).reshape(1, 1)
        z = _inproj(xa, row(norm1_g[l]), mod(l, 0), mod(l, 1), w_in_b, l, dims)
        gains = jnp.stack([tile8(diff_q_norm_g[l]), tile8(diff_k_norm_g[l]),
                           tile8(na_q_norm_g[l]), tile8(na_k_norm_g[l])] + [jnp.zeros((SEG,), F32)] * 4)
        rq, rk, dq, dk, nq, nk = _prep(z, gains, gmat, tables, dims)

        rf, rb = _retention(lg[l], rq, rk, z, dims)
        attn = (_diff_attention(lam, dq, dk, z, dims, True), _na_attention(nq, nk, z, bias, l, dims, True))
        rows = BS if last else BS + BC
        if not last:
            attn += (_diff_attention(lam, dq, dk, z, dims, False), _na_attention(nq, nk, z, None, l, dims, False))
        x1 = _merge(xa, mod(l, 2), rf, rb, z, attn, row(ret_norm_g[l]), row(diff_norm_g[l]),
                    w_ret_b, w_diff_b, w_na_b, w_out_b, l, 1.0 - lam_init, dims, rows)
        xa = _moe(x1, row(norm2_g[l]), mod(l, 3), mod(l, 4), mod(l, 5), wrt, br,
                  w_gate_b, w_up_b, w_down_b, l, dims, rows)
    return xa.reshape(B, S, D)
```

```python
import functools
import math

import numpy as np
import jax
import jax.numpy as jnp
from jax import lax
from jax.experimental import pallas as pl
from jax.experimental.pallas import tpu as pltpu

F32 = jnp.float32
BF16 = jnp.bfloat16
I32 = jnp.int32

GRID_W = 64
RET_HEADS, RET_DK, RET_DV, RET_CHUNK = 4, 128, 256, 128
DIFF_HEADS, DIFF_HD, DIFF_DV = 4, 64, 128
NA_HEADS, NA_HD, NA_KR, NA_KW = 8, 64, 8, 16
N_EXPERTS, N_GROUPS = 16, 4
EXPERTS_PER_GROUP = N_EXPERTS // N_GROUPS
ROPE_BASE = 10000.0
EPS = 1e-6
NEG = -1e30

SEG = 512
C_RQ, C_RK, C_RV, C_RG, C_DQ, C_DK, C_DV, C_NQ, C_NK, C_NV, C_GATES = 0, 1, 2, 4, 6, 7, 8, 9, 10, 11, 12
NA_QROWS = 8
NA_KROWS = 16
KEY_CHUNK = 256
ROW_BLOCK = 128

V7X_VMEM_BYTES = 64 * 1024 * 1024
VMEM_LIMIT = V7X_VMEM_BYTES - 8 * 1024 * 1024


def _cparams(sem):
    return pltpu.CompilerParams(dimension_semantics=sem, vmem_limit_bytes=VMEM_LIMIT)


def _sigmoid(x):
    return 1.0 / (1.0 + jnp.exp(-x))


def _tile(n, pref):
    t = min(n, pref)
    while n % t:
        t //= 2
    return t


def _dot_nt(a, b):
    return lax.dot_general(a, b, (((1,), (1,)), ((), ())), preferred_element_type=F32)


def _dot_tn(a, b):
    return lax.dot_general(a, b, (((0,), (0,)), ((), ())), preferred_element_type=F32)


def _dot(a, b):
    return jnp.dot(a, b, preferred_element_type=F32)


def _ada_kernel(c_ref, w_ref, b_ref, o_ref):
    c = c_ref[...]
    a = (c * _sigmoid(c)).astype(BF16)
    o_ref[0] = _dot(a, w_ref[0].astype(BF16)) + b_ref[0]


def _ada(c_all, w_ada, b_ada):
    L, D, N = w_ada.shape
    M = c_all.shape[0]
    tn = _tile(N, 1024)
    return pl.pallas_call(
        _ada_kernel,
        out_shape=jax.ShapeDtypeStruct((L, M, N), F32),
        grid=(L, N // tn),
        in_specs=[
            pl.BlockSpec((M, D), lambda l, j: (0, 0)),
            pl.BlockSpec((1, D, tn), lambda l, j: (l, 0, j)),
            pl.BlockSpec((1, 1, tn), lambda l, j: (l, 0, j)),
        ],
        out_specs=pl.BlockSpec((1, M, tn), lambda l, j: (l, 0, j)),
        compiler_params=_cparams(("parallel", "parallel")),
        name="ada",
    )(c_all, w_ada, b_ada.reshape(L, 1, N))


def _modnorm(x, g, shift, scale):
    ms = jnp.mean(x * x, axis=-1, keepdims=True)
    y = x * lax.rsqrt(ms + EPS) * g
    return y * (1.0 + scale) + shift


def _inproj_kernel(x_ref, g_ref, sh_ref, sc_ref, w_ref, o_ref, h_ref):
    @pl.when(pl.program_id(1) == 0)
    def _():
        h_ref[...] = _modnorm(x_ref[...], g_ref[...], sh_ref[0], sc_ref[0]).astype(BF16)

    o_ref[...] = _dot(h_ref[...], w_ref[0]).astype(o_ref.dtype)


def _mod_index(i, n_lat_tiles, tiles_per_batch, n_batch):
    return jnp.where(i < n_lat_tiles, i // tiles_per_batch, n_batch)


def _inproj(x, g, shift, scale, w, layer, dims):
    R, D = x.shape
    P = w.shape[2]
    tm = _tile(dims["BC"], 1024)
    tn = _tile(P, 1024)
    nlt, tpb, B = dims["BS"] // tm, dims["S"] // tm, dims["B"]
    mod = lambda i, j: (_mod_index(i, nlt, tpb, B), 0, 0)
    return pl.pallas_call(
        _inproj_kernel,
        out_shape=jax.ShapeDtypeStruct((R, P), BF16),
        grid=(R // tm, P // tn),
        in_specs=[
            pl.BlockSpec((tm, D), lambda i, j: (i, 0)),
            pl.BlockSpec((1, D), lambda i, j: (0, 0)),
            pl.BlockSpec((1, 1, D), mod),
            pl.BlockSpec((1, 1, D), mod),
            pl.BlockSpec((1, D, tn), lambda i, j: (layer, 0, j)),
        ],
        out_specs=pl.BlockSpec((tm, tn), lambda i, j: (i, j)),
        scratch_shapes=[pltpu.VMEM((tm, D), BF16)],
        compiler_params=_cparams(("parallel", "arbitrary")),
        name="inproj",
    )(x, g, shift, scale, w)


def _group_mean_sq(x, gmat):
    xx = x * x
    hi = xx.astype(BF16)
    lo = (xx - hi.astype(F32)).astype(BF16)
    return _dot(hi, gmat) + _dot(lo, gmat)


def _rope(x, cos, sin, half):
    w = x.shape[-1]
    lane = lax.broadcasted_iota(I32, x.shape, 1)
    first = (lane % (2 * half)) < half
    swapped = jnp.where(first, pltpu.roll(x, w - half, 1), pltpu.roll(x, half, 1))
    return x * cos + swapped * sin


def _prep_kernel(rq_ref, rk_ref, dq_ref, dk_ref, nq_ref, nk_ref, gains_ref, gmat_ref, tab_ref,
                 orq_ref, ork_ref, odq_ref, odk_ref, onq_ref, onk_ref):
    reps = SEG // 128
    cos_r = jnp.concatenate([tab_ref[0]] * reps, axis=1)
    sin_r = jnp.concatenate([tab_ref[1]] * reps, axis=1)
    cos_d = jnp.concatenate([tab_ref[2]] * reps, axis=1)
    sin_d = jnp.concatenate([tab_ref[3]] * reps, axis=1)
    gmat = gmat_ref[...]

    def normed(ref, row):
        x = ref[...].astype(F32)
        return x * lax.rsqrt(_group_mean_sq(x, gmat) + EPS) * gains_ref[row:row + 1, :]

    orq_ref[...] = _rope(rq_ref[...].astype(F32), cos_r, sin_r, RET_DK // 4).astype(BF16)
    ork_ref[...] = _rope(rk_ref[...].astype(F32) * (RET_DK ** -0.5), cos_r, sin_r, RET_DK // 4).astype(BF16)
    odq_ref[...] = (_rope(normed(dq_ref, 0), cos_d, sin_d, DIFF_HD // 4) * (DIFF_HD ** -0.5)).astype(BF16)
    odk_ref[...] = _rope(normed(dk_ref, 1), cos_d, sin_d, DIFF_HD // 4).astype(BF16)
    onq_ref[...] = (normed(nq_ref, 2) * (NA_HD ** -0.5)).astype(BF16)
    onk_ref[...] = normed(nk_ref, 3).astype(BF16)


def _prep(z, gains, gmat, tables, dims):
    R = z.shape[0]
    tm = _tile(dims["BC"], 512)
    nlt, tpb = dims["BS"] // tm, dims["S"] // tm
    zspec = lambda c: pl.BlockSpec((tm, SEG), lambda i: (i, c))
    tab = lambda i: (0, jnp.where(i < nlt, i % tpb, tpb), 0)
    ospec = pl.BlockSpec((tm, SEG), lambda i: (i, 0))
    return pl.pallas_call(
        _prep_kernel,
        out_shape=[jax.ShapeDtypeStruct((R, SEG), BF16)] * 6,
        grid=(R // tm,),
        in_specs=[zspec(C_RQ), zspec(C_RK), zspec(C_DQ), zspec(C_DK), zspec(C_NQ), zspec(C_NK),
                  pl.BlockSpec((8, SEG), lambda i: (0, 0)),
                  pl.BlockSpec((SEG, SEG), lambda i: (0, 0)),
                  pl.BlockSpec((4, tm, 128), tab)],
        out_specs=[ospec] * 6,
        compiler_params=_cparams(("parallel",)),
        name="prep",
    )(z, z, z, z, z, z, gains, gmat, tables)


def _rope_tables(S, tm):
    t = jnp.arange(S)
    row, col = t // GRID_W, t % GRID_W

    def one(d, width):
        inv = jnp.power(ROPE_BASE, -jnp.arange(0, d, 2, dtype=F32) / d)
        ar = row.astype(F32)[:, None] * inv[None, :]
        ac = col.astype(F32)[:, None] * inv[None, :]
        cos = jnp.concatenate([jnp.cos(ar), jnp.cos(ar), jnp.cos(ac), jnp.cos(ac)], axis=-1)
        sin = jnp.concatenate([-jnp.sin(ar), jnp.sin(ar), -jnp.sin(ac), jnp.sin(ac)], axis=-1)
        reps = width // cos.shape[-1]
        return jnp.tile(cos, (1, reps)), jnp.tile(sin, (1, reps))

    cr, sr = one(RET_DK // 2, 128)
    cd, sd = one(DIFF_HD // 2, 128)
    tab = jnp.stack([cr, sr, cd, sd])
    ident = jnp.stack([jnp.ones((tm, 128), F32), jnp.zeros((tm, 128), F32)] * 2)
    return jnp.concatenate([tab, ident], axis=1)


def _ret_kernel(lg_ref, qf_ref, kf_ref, vf_ref, qb_ref, kb_ref, vb_ref, of_ref, ob_ref, s_ref):
    C = RET_CHUNK

    @pl.when(pl.program_id(1) == 0)
    def _():
        s_ref[...] = jnp.zeros_like(s_ref)

    ii = lax.broadcasted_iota(I32, (C, C), 0).astype(F32)
    jj = lax.broadcasted_iota(I32, (C, C), 1).astype(F32)
    ri = lax.broadcasted_iota(I32, (C, RET_DK), 0).astype(F32)

    for h in range(RET_HEADS):
        qs = slice(h * RET_DK, (h + 1) * RET_DK)
        vs = slice(h * RET_DV, (h + 1) * RET_DV)
        for d, (q_ref, k_ref, v_ref, o_ref) in enumerate(((qf_ref, kf_ref, vf_ref, of_ref),
                                                         (qb_ref, kb_ref, vb_ref, ob_ref))):
            lg = lg_ref[d, h]
            q, k, v = q_ref[:, qs], k_ref[:, qs], v_ref[:, vs]
            if d == 0:
                dist = ii - jj
                q_dec = jnp.exp(lg * (ri + 1.0))
                k_dec = jnp.exp(lg * (C - 1.0 - ri))
            else:
                dist = jj - ii
                q_dec = jnp.exp(lg * (C - ri))
                k_dec = jnp.exp(lg * ri)
            decay = jnp.where(dist >= 0, jnp.exp(lg * jnp.maximum(dist, 0.0)), 0.0)
            att = (_dot_nt(q, k) * decay).astype(BF16)
            s = s_ref[d, h]
            o = _dot(att, v) + _dot((q.astype(F32) * q_dec).astype(BF16), s.astype(BF16))
            s_ref[d, h] = s * jnp.exp(lg * C) + _dot_tn((k.astype(F32) * k_dec).astype(BF16), v)
            o_ref[:, vs] = o.astype(o_ref.dtype)


def _retention(lg, rq, rk, z, dims):
    B, S, C, BS = dims["B"], dims["S"], dims["C"], dims["BS"]
    R = rq.shape[0]
    nc, nl = C // RET_CHUNK, S // RET_CHUNK
    base = BS // RET_CHUNK
    fwd = lambda b, c: jnp.where(c < nc, base + b * nc + c, b * nl + (c - nc))
    bwd = lambda b, c: jnp.where(c < nc, base + b * nc + (nc - 1 - c), b * nl + (nl - 1 - (c - nc)))
    vw = RET_HEADS * RET_DV
    spec = lambda w, col, f: pl.BlockSpec((RET_CHUNK, w), lambda b, c: (f(b, c), col))
    return pl.pallas_call(
        _ret_kernel,
        out_shape=[jax.ShapeDtypeStruct((R, vw), BF16), jax.ShapeDtypeStruct((R, vw), BF16)],
        grid=(B, nc + nl),
        in_specs=[pl.BlockSpec(memory_space=pltpu.SMEM),
                  spec(SEG, 0, fwd), spec(SEG, 0, fwd), spec(vw, C_RV * SEG // vw, fwd),
                  spec(SEG, 0, bwd), spec(SEG, 0, bwd), spec(vw, C_RV * SEG // vw, bwd)],
        out_specs=[spec(vw, 0, fwd), spec(vw, 0, bwd)],
        scratch_shapes=[pltpu.VMEM((2, RET_HEADS, RET_DK, RET_DV), F32)],
        compiler_params=_cparams(("parallel", "arbitrary")),
        name="retention",
    )(lg, rq, rk, z, rq, rk, z)


def _split_halves(q):
    lane = lax.broadcasted_iota(I32, q.shape, 1)
    lo = jnp.where(lane < 64, q, jnp.zeros_like(q))
    hi = jnp.where(lane >= 64, q, jnp.zeros_like(q))
    return jnp.concatenate([lo, hi], axis=0)


def _fold_lanes(x, op):
    acc = x[:, :128]
    for j in range(1, x.shape[1] // 128):
        acc = op(acc, x[:, j * 128:(j + 1) * 128])
    return acc


def _attend(q2, sources, lanes, s_ref, e_ref):
    chunks = []
    off = 0
    for k_ref, v_ref, row0, n_rows, bias_fn in sources:
        for c0 in range(0, n_rows, KEY_CHUNK):
            n = min(KEY_CHUNK, n_rows - c0)
            chunks.append((k_ref, v_ref, row0 + c0, n, off, c0, bias_fn))
            off += n
    aligned = lambda r: r if isinstance(r, int) else pl.multiple_of(r, 128)
    accs, sums = [], []
    for b, rb in enumerate(range(0, q2.shape[0], ROW_BLOCK)):
        q = q2[rb:rb + ROW_BLOCK]
        buf = b % 2
        m = None
        for k_ref, _, r0, n, off, c0, bias_fn in chunks:
            s = _dot_nt(q, k_ref[pl.ds(aligned(r0), n), lanes])
            if bias_fn is not None:
                s = s + bias_fn(rb, c0, n)
            s_ref[buf, :, off:off + n] = s
            blk = _fold_lanes(s, jnp.maximum)
            m = blk if m is None else jnp.maximum(m, blk)
        m = m.max(axis=-1, keepdims=True)
        l = jnp.zeros((ROW_BLOCK, 128), F32)
        for _, _, _, n, off, _, _ in chunks:
            e = jnp.exp(s_ref[buf, :, off:off + n] - m)
            l = l + _fold_lanes(e, jnp.add)
            e_ref[buf, :, off:off + n] = e.astype(BF16)
        acc, off = None, 0
        for _, v_ref, row0, n_rows, _ in sources:
            t = _dot(e_ref[buf, :, off:off + n_rows], v_ref[pl.ds(aligned(row0), n_rows), lanes])
            acc = t if acc is None else acc + t
            off += n_rows
        accs.append(acc)
        sums.append(l.sum(axis=-1, keepdims=True))
    return jnp.concatenate(accs, axis=0), jnp.concatenate(sums, axis=0)


def _diff_kernel(lam_ref, q_ref, *refs, has_latent):
    if has_latent:
        kl_ref, vl_ref, kc_ref, vc_ref, o_ref, s_ref, e_ref = refs
        sources = ((kl_ref, vl_ref, 0, kl_ref.shape[0], None), (kc_ref, vc_ref, 0, kc_ref.shape[0], None))
    else:
        kc_ref, vc_ref, o_ref, s_ref, e_ref = refs
        sources = ((kc_ref, vc_ref, 0, kc_ref.shape[0], None),)
    lam = lam_ref[0, 0]
    tq = q_ref.shape[0]
    for h in range(DIFF_HEADS):
        hs = slice(h * 128, (h + 1) * 128)
        acc, l = _attend(_split_halves(q_ref[:, hs]), sources, hs, s_ref, e_ref)
        o = acc[:tq] * (1.0 / l[:tq]) - acc[tq:] * (lam / l[tq:])
        o_ref[:, hs] = o.astype(o_ref.dtype)


def _diff_attention(lam, dq, dk, z, dims, latent):
    B, S, C, BS = dims["B"], dims["S"], dims["C"], dims["BS"]
    lat_k = pl.BlockSpec((S, SEG), lambda b, i: (b, 0))
    lat_v = pl.BlockSpec((S, SEG), lambda b, i: (b, C_DV))
    ctx_k = pl.BlockSpec((C, SEG), lambda b, i: (BS // C + b, 0))
    ctx_v = pl.BlockSpec((C, SEG), lambda b, i: (BS // C + b, C_DV))
    if latent:
        tq = _tile(S, 256)
        nq = S // tq
        qmap = omap = lambda b, i: (b * nq + i, 0)
        in_specs = [lat_k, lat_v, ctx_k, ctx_v]
        args = (dk, z, dk, z)
        n_keys = S + C
    else:
        tq, nq = C, 1
        qmap = lambda b, i: (BS // C + b, 0)
        omap = lambda b, i: (b, 0)
        in_specs = [ctx_k, ctx_v]
        args = (dk, z)
        n_keys = C
    return pl.pallas_call(
        functools.partial(_diff_kernel, has_latent=latent),
        out_shape=jax.ShapeDtypeStruct((B * nq * tq, SEG), BF16),
        grid=(B, nq),
        in_specs=[pl.BlockSpec(memory_space=pltpu.SMEM), pl.BlockSpec((tq, SEG), qmap)] + in_specs,
        out_specs=pl.BlockSpec((tq, SEG), omap),
        scratch_shapes=[pltpu.VMEM((2, ROW_BLOCK, n_keys), F32), pltpu.VMEM((2, ROW_BLOCK, n_keys), BF16)],
        compiler_params=_cparams(("parallel", "arbitrary")),
        name="diff_lat" if latent else "diff_ctx",
    )(lam, dq, *args)


def _na_kernel(q_ref, *refs, has_local, rows):
    if has_local:
        kl_ref, vl_ref, kc_ref, vc_ref, bias_ref, o_ref = refs
        qb = pl.program_id(1)
        kr0 = jnp.clip(qb * NA_QROWS - (NA_KROWS - NA_QROWS) // 2, 0, rows - NA_KROWS)
        k0 = pl.multiple_of(kr0 * GRID_W, 256)
        nk = NA_KROWS * GRID_W
    else:
        kc_ref, vc_ref, o_ref = refs
    tq = q_ref.shape[0]
    lane = lax.broadcasted_iota(I32, (tq, 128), 1)
    for p in range(NA_HEADS // 2):
        ps = slice(p * 128, (p + 1) * 128)
        q2 = _split_halves(q_ref[:, ps])
        parts = [_dot_nt(q2, kc_ref[:, ps])]
        vals = [vc_ref[:, ps]]
        if has_local:
            s_loc = _dot_nt(q2, kl_ref[pl.ds(k0, nk), ps])
            bias = jnp.concatenate([bias_ref[0, 2 * p], bias_ref[0, 2 * p + 1]], axis=0).astype(F32)
            parts.append(s_loc + bias)
            vals.append(vl_ref[pl.ds(k0, nk), ps])
        m = parts[0].max(axis=-1, keepdims=True)
        for s in parts[1:]:
            m = jnp.maximum(m, s.max(axis=-1, keepdims=True))
        es = [jnp.exp(s - m) for s in parts]
        den = es[0].sum(axis=-1, keepdims=True)
        for e in es[1:]:
            den = den + e.sum(axis=-1, keepdims=True)
        rden = 1.0 / den
        o = None
        for e, v in zip(es, vals):
            t = _dot((e * rden).astype(BF16), v)
            o = t if o is None else o + t
        o_ref[:, ps] = jnp.where(lane < 64, o[:tq], o[tq:]).astype(o_ref.dtype)


def _na_attention(nq_, nk_, z, bias, layer, dims, latent):
    B, S, C, BS = dims["B"], dims["S"], dims["C"], dims["BS"]
    ctx_k = pl.BlockSpec((C, SEG), lambda b, i: (BS // C + b, 0))
    ctx_v = pl.BlockSpec((C, SEG), lambda b, i: (BS // C + b, C_NV))
    if latent:
        tq = NA_QROWS * GRID_W
        nq = S // tq
        qmap = omap = lambda b, i: (b * nq + i, 0)
        btype = lambda b, i: (3 * layer + jnp.where(i == 0, 0, jnp.where(i == nq - 1, 2, 1)), 0, 0, 0)
        in_specs = [pl.BlockSpec((S, SEG), lambda b, i: (b, 0)),
                    pl.BlockSpec((S, SEG), lambda b, i: (b, C_NV)),
                    ctx_k, ctx_v,
                    pl.BlockSpec((1, NA_HEADS, tq, NA_KROWS * GRID_W), btype)]
        args = (nk_, z, nk_, z, bias)
    else:
        tq, nq = C, 1
        qmap = lambda b, i: (BS // C + b, 0)
        omap = lambda b, i: (b, 0)
        in_specs = [ctx_k, ctx_v]
        args = (nk_, z)
    return pl.pallas_call(
        functools.partial(_na_kernel, has_local=latent, rows=S // GRID_W),
        out_shape=jax.ShapeDtypeStruct((B * nq * tq, SEG), BF16),
        grid=(B, nq),
        in_specs=[pl.BlockSpec((tq, SEG), qmap)] + in_specs,
        out_specs=pl.BlockSpec((tq, SEG), omap),
        compiler_params=_cparams(("parallel", "arbitrary")),
        name="na_lat" if latent else "na_ctx",
    )(nq_, *args)


def _na_bias_tables(rpb, S):
    rows = S // GRID_W
    nqb = rows // NA_QROWS
    i = np.arange(NA_QROWS)[:, None]
    j = np.arange(NA_KROWS)[None, :]
    sel_r = np.zeros((3, NA_QROWS, NA_KROWS, 2 * NA_KR - 1), np.float32)
    for t, qb in enumerate((0, min(1, nqb - 1), nqb - 1)):
        r = qb * NA_QROWS + i
        kr0 = np.clip(qb * NA_QROWS - (NA_KROWS - NA_QROWS) // 2, 0, rows - NA_KROWS)
        kr = kr0 + j
        r0 = np.clip(r - NA_KR // 2, 0, rows - NA_KR)
        ok = (kr >= r0) & (kr < r0 + NA_KR)
        off = np.clip(kr - r + NA_KR - 1, 0, 2 * NA_KR - 2)
        sel_r[t][np.broadcast_to(i, ok.shape)[ok], np.broadcast_to(j, ok.shape)[ok], off[ok]] = 1.0
    cq = np.arange(GRID_W)[:, None]
    ck = np.arange(GRID_W)[None, :]
    c0 = np.clip(cq - NA_KW // 2, 0, GRID_W - NA_KW)
    okc = (ck >= c0) & (ck < c0 + NA_KW)
    offc = np.clip(ck - cq + NA_KW - 1, 0, 2 * NA_KW - 2)
    sel_c = np.zeros((GRID_W, GRID_W, 2 * NA_KW - 1), np.float32)
    sel_c[np.broadcast_to(cq, okc.shape)[okc], np.broadcast_to(ck, okc.shape)[okc], offc[okc]] = 1.0
    valid = sel_r.sum(-1)[:, :, None, :, None] * sel_c.sum(-1)[None, None, :, None, :]
    hp = lax.Precision.HIGHEST
    m1 = jnp.einsum("lhab,qkb->lhaqk", rpb.astype(F32), jnp.asarray(sel_c), precision=hp)
    tab = jnp.einsum("tija,lhaqk->lthiqjk", jnp.asarray(sel_r), m1, precision=hp)
    tab = jnp.where(jnp.asarray(valid)[None, :, None] > 0.5, tab, NEG)
    L, H = rpb.shape[:2]
    return tab.reshape(L * 3, H, NA_QROWS * GRID_W, NA_KROWS * GRID_W).astype(BF16)


def _head_norm(x, width, gain):
    outs = []
    for h in range(x.shape[-1] // width):
        xh = x[:, h * width:(h + 1) * width]
        outs.append(xh * lax.rsqrt(jnp.mean(xh * xh, axis=-1, keepdims=True) + EPS))
    return jnp.concatenate(outs, axis=-1) * gain


def _merge_kernel(x_ref, gate_ref, rf_ref, rb_ref, rg_ref, *refs, diff_scale, n_lat_tiles):
    if n_lat_tiles is None:
        od_ref, on_ref = refs[:2]
        od, on = od_ref[...], on_ref[...]
    else:
        odl_ref, onl_ref, odc_ref, onc_ref = refs[:4]
        is_lat = pl.program_id(0) < n_lat_tiles
        od = jnp.where(is_lat, odl_ref[...], odc_ref[...])
        on = jnp.where(is_lat, onl_ref[...], onc_ref[...])
    g1_ref, g2_ref, g3_ref, retg_ref, dg_ref, wr_ref, wd_ref, wn_ref, wo_ref, o_ref = refs[-10:]
    r = _head_norm(rf_ref[...].astype(F32) + rb_ref[...].astype(F32), RET_DV, retg_ref[...])
    rg = rg_ref[...].astype(F32)
    y_ret = _dot((rg * _sigmoid(rg) * r).astype(BF16), wr_ref[0])
    d = _head_norm(od.astype(F32), DIFF_DV, dg_ref[...]) * diff_scale
    y_diff = _dot(d.astype(BF16), wd_ref[0])
    y_na = _dot(on, wn_ref[0])
    m = (_sigmoid(g1_ref[...].astype(F32)) * y_ret + _sigmoid(g2_ref[...].astype(F32)) * y_diff
         + _sigmoid(g3_ref[...].astype(F32)) * y_na)
    y = _dot(m.astype(BF16), wo_ref[0])
    o_ref[...] = x_ref[...] + gate_ref[0] * y


def _merge(x, gate, rf, rb, z, attn, ret_g, d_g, wr, wd, wn, wo, layer, diff_scale, dims, rows):
    D = x.shape[1]
    tm = _tile(dims["BC"], 256)
    nlt, tpb, B = dims["BS"] // tm, dims["S"] // tm, dims["B"]
    rw = RET_HEADS * RET_DV
    gcol = C_GATES * SEG // D
    row = lambda w, col=0: pl.BlockSpec((tm, w), lambda i: (i, col))
    lat = pl.BlockSpec((tm, SEG), lambda i: (jnp.minimum(i, nlt - 1), 0))
    ctx = pl.BlockSpec((tm, SEG), lambda i: (jnp.maximum(i - nlt, 0), 0))
    const = lambda a: pl.BlockSpec(a.shape, lambda i: (0,) * a.ndim)
    wspec = lambda a: pl.BlockSpec((1,) + a.shape[1:], lambda i: (layer, 0, 0))
    with_ctx = len(attn) == 4
    return pl.pallas_call(
        functools.partial(_merge_kernel, diff_scale=diff_scale, n_lat_tiles=nlt if with_ctx else None),
        out_shape=jax.ShapeDtypeStruct((rows, D), F32),
        grid=(rows // tm,),
        in_specs=[row(D), pl.BlockSpec((1, 1, D), lambda i: (_mod_index(i, nlt, tpb, B), 0, 0)),
                  row(rw), row(rw), row(rw, C_RG * SEG // rw)]
                 + ([lat, lat, ctx, ctx] if with_ctx else [row(SEG), row(SEG)])
                 + [row(D, gcol), row(D, gcol + 1), row(D, gcol + 2),
                    const(ret_g), const(d_g), wspec(wr), wspec(wd), wspec(wn), wspec(wo)],
        out_specs=row(D),
        compiler_params=_cparams(("parallel",)),
        name="merge",
    )(x, gate, rf, rb, z, *attn, z, z, z, ret_g, d_g, wr, wd, wn, wo)


def _to_token_tiles(ref, x):
    rows, width = x.shape
    n = width // 128
    for c in range(n):
        ref[pl.ds(c, rows, stride=n), :] = x[:, c * 128:(c + 1) * 128]


def _from_token_tiles(ref, rows):
    n = ref.shape[0] // rows
    return jnp.concatenate([ref[pl.ds(c, rows, stride=n), :] for c in range(n)], axis=1)


def _router_kernel(x_ref, g_ref, sh_ref, sc_ref, wrt_ref, br_ref, h_ref, idx_ref, wts_ref, cnt_ref, run_ref):
    i = pl.program_id(0)
    tm = x_ref.shape[0]
    E = N_EXPERTS

    @pl.when(i == 0)
    def _():
        run_ref[...] = jnp.zeros_like(run_ref)

    h = _modnorm(x_ref[...], g_ref[...], sh_ref[0], sc_ref[0])
    _to_token_tiles(h_ref, h)
    h_hi = h.astype(BF16)
    h_lo = (h - h_hi.astype(F32)).astype(BF16)
    w = wrt_ref[...]
    w_hi = w.astype(BF16)
    w_lo = (w - w_hi.astype(F32)).astype(BF16)
    logits = _dot_nt(w_hi, h_hi) + _dot_nt(w_hi, h_lo) + _dot_nt(w_lo, h_hi)
    s = _sigmoid(logits)
    sel = s + br_ref[...]

    best, gi = None, None
    for g in range(N_GROUPS):
        r = [sel[g * EXPERTS_PER_GROUP + k:g * EXPERTS_PER_GROUP + k + 1] for k in range(EXPERTS_PER_GROUP)]
        top2 = None
        for a in range(EXPERTS_PER_GROUP):
            for b in range(a + 1, EXPERTS_PER_GROUP):
                pair = r[a] + r[b]
                top2 = pair if top2 is None else jnp.maximum(top2, pair)
        if best is None:
            best, gi = top2, jnp.zeros(top2.shape, I32)
        else:
            upd = top2 > best
            gi = jnp.where(upd, g, gi)
            best = jnp.where(upd, top2, best)

    eid = lax.broadcasted_iota(I32, (E, tm), 0)
    masked = jnp.where(eid // EXPERTS_PER_GROUP == gi, sel, -jnp.inf)
    m1 = masked.max(axis=0, keepdims=True)
    i1 = jnp.where(masked == m1, eid, E).min(axis=0, keepdims=True)
    masked2 = jnp.where(eid == i1, -jnp.inf, masked)
    m2 = masked2.max(axis=0, keepdims=True)
    i2 = jnp.where(masked2 == m2, eid, E).min(axis=0, keepdims=True)
    oh1 = eid == i1
    oh2 = eid == i2
    w1 = jnp.where(oh1, s, 0.0).sum(axis=0, keepdims=True)
    w2 = jnp.where(oh2, s, 0.0).sum(axis=0, keepdims=True)
    tot = w1 + w2

    oh = (oh1 | oh2).astype(F32)
    before = (lax.broadcasted_iota(I32, (tm, tm), 0) < lax.broadcasted_iota(I32, (tm, tm), 1)).astype(BF16)
    prefix = _dot(oh.astype(BF16), before) + run_ref[:, 0:1]
    rank1 = jnp.where(oh1, prefix, 0.0).sum(axis=0, keepdims=True)
    rank2 = jnp.where(oh2, prefix, 0.0).sum(axis=0, keepdims=True)
    run_ref[...] = run_ref[...] + oh.sum(axis=1, keepdims=True)

    idx_ref[...] = jnp.zeros_like(idx_ref)
    wts_ref[...] = jnp.zeros_like(wts_ref)
    for k, v in enumerate((i1, i2, rank1.astype(I32), rank2.astype(I32))):
        idx_ref[0, k:k + 1, :] = v
    wts_ref[0, 0:1, :] = w1 / tot
    wts_ref[0, 1:2, :] = w2 / tot
    cnt_ref[...] = run_ref[...]


def _router(x, g, shift, scale, wrt, br, dims, rows):
    D = x.shape[1]
    tm = _tile(dims["BC"], 256)
    nt = rows // tm
    nlt, tpb, B = dims["BS"] // tm, dims["S"] // tm, dims["B"]
    mod = lambda i: (_mod_index(i, nlt, tpb, B), 0, 0)
    n = D // 128
    return pl.pallas_call(
        _router_kernel,
        out_shape=[jax.ShapeDtypeStruct((rows * n, 128), F32),
                   jax.ShapeDtypeStruct((nt, 8, tm), I32),
                   jax.ShapeDtypeStruct((nt, 8, tm), F32),
                   jax.ShapeDtypeStruct((N_EXPERTS, 128), F32)],
        grid=(nt,),
        in_specs=[pl.BlockSpec((tm, D), lambda i: (i, 0)),
                  pl.BlockSpec((1, D), lambda i: (0, 0)),
                  pl.BlockSpec((1, 1, D), mod), pl.BlockSpec((1, 1, D), mod),
                  pl.BlockSpec((N_EXPERTS, D), lambda i: (0, 0)),
                  pl.BlockSpec((N_EXPERTS, 1), lambda i: (0, 0))],
        out_specs=[pl.BlockSpec((tm * n, 128), lambda i: (i, 0)),
                   pl.BlockSpec((1, 8, tm), lambda i: (i, 0, 0)),
                   pl.BlockSpec((1, 8, tm), lambda i: (i, 0, 0)),
                   pl.BlockSpec((N_EXPERTS, 128), lambda i: (0, 0))],
        scratch_shapes=[pltpu.VMEM((N_EXPERTS, 128), F32)],
        compiler_params=_cparams(("arbitrary",)),
        name="router",
    )(x, g, shift, scale, wrt, br)


GATHER_UNROLL = 8


def _token_copy(r, n, idx_ref, src_hbm, dst_ref, sem):
    src = pl.multiple_of(idx_ref[0, 0, r] * n, n)
    dst = pl.multiple_of(r * n, n)
    return pltpu.make_async_copy(src_hbm.at[pl.ds(src, n)], dst_ref.at[pl.ds(dst, n)], sem)


def _start_gather(n_tok, n, idx_ref, src_hbm, dst_ref, sem):
    def body(g, carry):
        for u in range(GATHER_UNROLL):
            _token_copy(g * GATHER_UNROLL + u, n, idx_ref, src_hbm, dst_ref, sem).start(priority=u % 2)
        return carry

    lax.fori_loop(0, n_tok // GATHER_UNROLL, body, 0)


def _wait_gather(n_tok, n, idx_ref, src_hbm, dst_ref, sem):
    def body(g, carry):
        for u in range(GATHER_UNROLL):
            _token_copy(g * GATHER_UNROLL + u, n, idx_ref, src_hbm, dst_ref, sem).wait()
        return carry

    lax.fori_loop(0, n_tok // GATHER_UNROLL, body, 0)


def _pipelined_gathers(streams, n_tok, n):
    t = pl.program_id(0)
    last = pl.num_programs(0) - 1
    slot = lax.rem(t, 2)

    @pl.when(t == 0)
    def _():
        for cur, _, src, buf, sem in streams:
            _start_gather(n_tok, n, cur, src, buf.at[0], sem.at[0])

    @pl.when(t < last)
    def _():
        for _, nxt, src, buf, sem in streams:
            _start_gather(n_tok, n, nxt, src, buf.at[1 - slot], sem.at[1 - slot])

    for cur, _, src, buf, sem in streams:
        _wait_gather(n_tok, n, cur, src, buf.at[slot], sem.at[slot])
    return slot


def _expert_kernel(te_ref, cur_ref, nxt_ref, h_hbm, wg_ref, wu_ref, wd_ref, o_ref, x_buf, sem, *, rows, n):
    slot = _pipelined_gathers(((cur_ref, nxt_ref, h_hbm, x_buf, sem),), rows, n)
    x = _from_token_tiles(x_buf.at[slot], rows).astype(BF16)
    gate = _dot(x, wg_ref[0, 0])
    a = gate * _sigmoid(gate) * _dot(x, wu_ref[0, 0])
    _to_token_tiles(o_ref, _dot(a.astype(BF16), wd_ref[0, 0]))


def _experts(tile_expert, src_rows, h, wg, wu, wd, layer, tmE):
    D, De = wg.shape[2:]
    n = D // 128
    P = src_rows.shape[0]
    nt = P // tmE
    wspec = lambda a, b: pl.BlockSpec((1, 1, a, b), lambda t, te: (layer, te[t], 0, 0))
    idx = src_rows.reshape(nt, 1, tmE)
    cur = pl.BlockSpec((1, 1, tmE), lambda t, te: (t, 0, 0), memory_space=pltpu.SMEM)
    nxt = pl.BlockSpec((1, 1, tmE), lambda t, te: (jnp.minimum(t + 1, nt - 1), 0, 0), memory_space=pltpu.SMEM)
    return pl.pallas_call(
        functools.partial(_expert_kernel, rows=tmE, n=n),
        out_shape=jax.ShapeDtypeStruct((P * n, 128), F32),
        grid_spec=pltpu.PrefetchScalarGridSpec(
            num_scalar_prefetch=1,
            grid=(nt,),
            in_specs=[cur, nxt, pl.BlockSpec(memory_space=pl.ANY), wspec(D, De), wspec(D, De), wspec(De, D)],
            out_specs=pl.BlockSpec((tmE * n, 128), lambda t, te: (t, 0)),
            scratch_shapes=[pltpu.VMEM((2, tmE * n, 128), F32), pltpu.SemaphoreType.DMA((2,))]),
        compiler_params=_cparams(("arbitrary",)),
        name="experts",
    )(tile_expert, idx, idx, h, wg, wu, wd)


def _combine_kernel(c1_ref, n1_ref, c2_ref, n2_ref, x_ref, gate_ref, w_ref, ye_hbm, o_ref,
                    b1_ref, b2_ref, sem1, sem2):
    rows, width = x_ref.shape
    n = width // 128
    slot = _pipelined_gathers(((c1_ref, n1_ref, ye_hbm, b1_ref, sem1), (c2_ref, n2_ref, ye_hbm, b2_ref, sem2)),
                              rows, n)
    w = w_ref[...]
    y = (w[:, 0:1] * _from_token_tiles(b1_ref.at[slot], rows)
         + w[:, 1:2] * _from_token_tiles(b2_ref.at[slot], rows))
    o_ref[...] = x_ref[...] + gate_ref[0] * y


def _combine(pos1, pos2, x, gate, w, ye, dims, rows):
    D = x.shape[1]
    n = D // 128
    tc = _tile(dims["BC"], 256)
    nt = rows // tc
    nlt, tpb, B = dims["BS"] // tc, dims["S"] // tc, dims["B"]
    cur = pl.BlockSpec((1, 1, tc), lambda i: (i, 0, 0), memory_space=pltpu.SMEM)
    nxt = pl.BlockSpec((1, 1, tc), lambda i: (jnp.minimum(i + 1, nt - 1), 0, 0), memory_space=pltpu.SMEM)
    p1, p2 = pos1.reshape(nt, 1, tc), pos2.reshape(nt, 1, tc)
    return pl.pallas_call(
        _combine_kernel,
        out_shape=jax.ShapeDtypeStruct((rows, D), F32),
        grid=(nt,),
        in_specs=[cur, nxt, cur, nxt,
                  pl.BlockSpec((tc, D), lambda i: (i, 0)),
                  pl.BlockSpec((1, 1, D), lambda i: (_mod_index(i, nlt, tpb, B), 0, 0)),
                  pl.BlockSpec((tc, 2), lambda i: (i, 0)),
                  pl.BlockSpec(memory_space=pl.ANY)],
        out_specs=pl.BlockSpec((tc, D), lambda i: (i, 0)),
        scratch_shapes=[pltpu.VMEM((2, tc * n, 128), F32), pltpu.VMEM((2, tc * n, 128), F32),
                        pltpu.SemaphoreType.DMA((2,)), pltpu.SemaphoreType.DMA((2,))],
        compiler_params=_cparams(("arbitrary",)),
        name="combine",
    )(p1, p1, p2, p2, x, gate, w, ye)


def _moe(x, g, shift, scale, gate, wrt, br, wg, wu, wd, layer, dims, rows):
    tmE = 256
    h, idx, wts, cnt = _router(x, g, shift, scale, wrt, br, dims, rows)
    nt = idx.shape[0]
    flat = lambda a, k: a[:, k, :].reshape(rows)
    e1, e2, r1, r2 = (flat(idx, k) for k in range(4))
    counts = cnt[:, 0].astype(I32)
    padded = (counts + tmE - 1) // tmE * tmE
    ends = jnp.cumsum(padded)
    starts = ends - padded
    pos1 = starts[e1] + r1
    pos2 = starts[e2] + r2
    P = (2 * rows + N_EXPERTS * tmE) // tmE * tmE
    tok = jnp.arange(rows, dtype=I32)
    src = jnp.zeros((P,), I32).at[jnp.concatenate([pos1, pos2])].set(
        jnp.concatenate([tok, tok]), unique_indices=True, mode="promise_in_bounds")
    tile_start = jnp.arange(P // tmE, dtype=I32) * tmE
    tile_expert = jnp.minimum(jnp.sum(ends[None, :] <= tile_start[:, None], axis=1), N_EXPERTS - 1).astype(I32)
    ye = _experts(tile_expert, src, h, wg, wu, wd, layer, tmE)
    w = jnp.stack([flat(wts, 0), flat(wts, 1)], axis=-1)
    return _combine(pos1, pos2, x, gate, w, ye, dims, rows)


def kernel(x, c, ctx, c_ctx, w_ada, b_ada, norm1_g, norm2_g, w_in, ret_decay_logit, ret_norm_g, diff_q_norm_g, diff_k_norm_g, diff_lambda, diff_norm_g, na_q_norm_g, na_k_norm_g, na_rpb, w_ret_out, w_diff_out, w_na_out, w_out, w_router, b_router, w_exp_gate, w_exp_up, w_exp_down):
    B, S, D = x.shape
    C = ctx.shape[1]
    L = w_in.shape[0]
    BS, BC = B * S, B * C
    assert S % (NA_QROWS * GRID_W) == 0 and S // GRID_W >= NA_KROWS and C % RET_CHUNK == 0
    assert (C_GATES * SEG) % D == 0 and w_in.shape[2] == C_GATES * SEG + 3 * D
    dims = dict(B=B, S=S, C=C, BS=BS, BC=BC)

    n_c = -(-(B + 1) // 8) * 8
    c_all = jnp.concatenate([c, c_ctx[None], jnp.zeros((n_c - B - 1, D), F32)], axis=0)
    mods = _ada(c_all, w_ada, b_ada)[:, :B + 1]
    mod = lambda l, k: mods[l, :, k * D:(k + 1) * D].reshape(B + 1, 1, D)

    xa = jnp.concatenate([x.reshape(BS, D), ctx.reshape(BC, D)], axis=0)

    tm_prep = _tile(BC, 512)
    tables = _rope_tables(S, tm_prep)
    gmat = jnp.asarray(np.kron(np.eye(SEG // 64), np.full((64, 64), 1.0 / 64)), BF16)
    bias = _na_bias_tables(na_rpb, S)
    lg = jax.nn.log_sigmoid(ret_decay_logit.astype(F32))
    lam_p = diff_lambda.astype(F32)
    wrt = w_router.T
    br = b_router.astype(F32).reshape(N_EXPERTS, 1)
    tile8 = lambda v: jnp.tile(v.astype(F32), SEG // v.shape[0])
    row = lambda v: v.astype(F32).reshape(1, -1)
    w_in_b, w_ret_b, w_diff_b, w_na_b, w_out_b, w_gate_b, w_up_b, w_down_b = (
        w.astype(BF16) for w in (w_in, w_ret_out, w_diff_out, w_na_out, w_out, w_exp_gate, w_exp_up, w_exp_down))

    for l in range(L):
        last = l == L - 1
        lam_init = 0.8 - 0.6 * math.exp(-0.3 * l)
        lam = (jnp.exp(jnp.sum(lam_p[l, 0] * lam_p[l, 1])) - jnp.exp(jnp.sum(lam_p[l, 2] * lam_p[l, 3]))
               + lam_init).reshape(1, 1)
        z = _inproj(xa, row(norm1_g[l]), mod(l, 0), mod(l, 1), w_in_b, l, dims)
        gains = jnp.stack([tile8(diff_q_norm_g[l]), tile8(diff_k_norm_g[l]),
                           tile8(na_q_norm_g[l]), tile8(na_k_norm_g[l])] + [jnp.zeros((SEG,), F32)] * 4)
        rq, rk, dq, dk, nq, nk = _prep(z, gains, gmat, tables, dims)

        rf, rb = _retention(lg[l], rq, rk, z, dims)
        attn = (_diff_attention(lam, dq, dk, z, dims, True), _na_attention(nq, nk, z, bias, l, dims, True))
        rows = BS if last else BS + BC
        if not last:
            attn += (_diff_attention(lam, dq, dk, z, dims, False), _na_attention(nq, nk, z, None, l, dims, False))
        x1 = _merge(xa, mod(l, 2), rf, rb, z, attn, row(ret_norm_g[l]), row(diff_norm_g[l]),
                    w_ret_b, w_diff_b, w_na_b, w_out_b, l, 1.0 - lam_init, dims, rows)
        xa = _moe(x1, row(norm2_g[l]), mod(l, 3), mod(l, 4), mod(l, 5), wrt, br,
                  w_gate_b, w_up_b, w_down_b, l, dims, rows)
    return xa.reshape(B, S, D)
```

```python
import functools
import math

import numpy as np
import jax
import jax.numpy as jnp
from jax import lax
from jax.experimental import pallas as pl
from jax.experimental.pallas import tpu as pltpu

F32 = jnp.float32
BF16 = jnp.bfloat16
I32 = jnp.int32

GRID_W = 64
RET_HEADS, RET_DK, RET_DV, RET_CHUNK = 4, 128, 256, 128
DIFF_HEADS, DIFF_HD, DIFF_DV = 4, 64, 128
NA_HEADS, NA_HD, NA_KR, NA_KW = 8, 64, 8, 16
N_EXPERTS, N_GROUPS = 16, 4
EXPERTS_PER_GROUP = N_EXPERTS // N_GROUPS
ROPE_BASE = 10000.0
EPS = 1e-6
NEG = -1e30

SEG = 512
C_RQ, C_RK, C_RV, C_RG, C_DQ, C_DK, C_DV, C_NQ, C_NK, C_NV, C_GATES = 0, 1, 2, 4, 6, 7, 8, 9, 10, 11, 12
NA_QROWS = 8
NA_KROWS = 16
KEY_CHUNK = 256
ROW_BLOCK = 256

V7X_VMEM_BYTES = 64 * 1024 * 1024
VMEM_LIMIT = V7X_VMEM_BYTES - 8 * 1024 * 1024


def _cparams(sem):
    return pltpu.CompilerParams(dimension_semantics=sem, vmem_limit_bytes=VMEM_LIMIT)


def _sigmoid(x):
    return 1.0 / (1.0 + jnp.exp(-x))


def _tile(n, pref):
    t = min(n, pref)
    while n % t:
        t //= 2
    return t


def _dot_nt(a, b):
    return lax.dot_general(a, b, (((1,), (1,)), ((), ())), preferred_element_type=F32)


def _dot_tn(a, b):
    return lax.dot_general(a, b, (((0,), (0,)), ((), ())), preferred_element_type=F32)


def _dot(a, b):
    return jnp.dot(a, b, preferred_element_type=F32)


def _ada_kernel(c_ref, w_ref, b_ref, o_ref):
    c = c_ref[...]
    a = (c * _sigmoid(c)).astype(BF16)
    o_ref[0] = _dot(a, w_ref[0].astype(BF16)) + b_ref[0]


def _ada(c_all, w_ada, b_ada):
    L, D, N = w_ada.shape
    M = c_all.shape[0]
    tn = _tile(N, 1024)
    return pl.pallas_call(
        _ada_kernel,
        out_shape=jax.ShapeDtypeStruct((L, M, N), F32),
        grid=(L, N // tn),
        in_specs=[
            pl.BlockSpec((M, D), lambda l, j: (0, 0)),
            pl.BlockSpec((1, D, tn), lambda l, j: (l, 0, j)),
            pl.BlockSpec((1, 1, tn), lambda l, j: (l, 0, j)),
        ],
        out_specs=pl.BlockSpec((1, M, tn), lambda l, j: (l, 0, j)),
        compiler_params=_cparams(("parallel", "parallel")),
        name="ada",
    )(c_all, w_ada, b_ada.reshape(L, 1, N))


def _modnorm(x, g, shift, scale):
    ms = jnp.mean(x * x, axis=-1, keepdims=True)
    y = x * lax.rsqrt(ms + EPS) * g
    return y * (1.0 + scale) + shift


def _inproj_kernel(x_ref, g_ref, sh_ref, sc_ref, w_ref, o_ref, h_ref):
    @pl.when(pl.program_id(1) == 0)
    def _():
        h_ref[...] = _modnorm(x_ref[...], g_ref[...], sh_ref[0], sc_ref[0]).astype(BF16)

    o_ref[...] = _dot(h_ref[...], w_ref[0]).astype(o_ref.dtype)


def _mod_index(i, n_lat_tiles, tiles_per_batch, n_batch):
    return jnp.where(i < n_lat_tiles, i // tiles_per_batch, n_batch)


def _inproj(x, g, shift, scale, w, layer, dims):
    R, D = x.shape
    P = w.shape[2]
    tm = _tile(dims["BC"], 1024)
    tn = _tile(P, 1024)
    nlt, tpb, B = dims["BS"] // tm, dims["S"] // tm, dims["B"]
    mod = lambda i, j: (_mod_index(i, nlt, tpb, B), 0, 0)
    return pl.pallas_call(
        _inproj_kernel,
        out_shape=jax.ShapeDtypeStruct((R, P), BF16),
        grid=(R // tm, P // tn),
        in_specs=[
            pl.BlockSpec((tm, D), lambda i, j: (i, 0)),
            pl.BlockSpec((1, D), lambda i, j: (0, 0)),
            pl.BlockSpec((1, 1, D), mod),
            pl.BlockSpec((1, 1, D), mod),
            pl.BlockSpec((1, D, tn), lambda i, j: (layer, 0, j)),
        ],
        out_specs=pl.BlockSpec((tm, tn), lambda i, j: (i, j)),
        scratch_shapes=[pltpu.VMEM((tm, D), BF16)],
        compiler_params=_cparams(("parallel", "arbitrary")),
        name="inproj",
    )(x, g, shift, scale, w)


def _group_mean_sq(x, gmat):
    xx = x * x
    hi = xx.astype(BF16)
    lo = (xx - hi.astype(F32)).astype(BF16)
    return _dot(hi, gmat) + _dot(lo, gmat)


def _rope(x, cos, sin, half):
    w = x.shape[-1]
    lane = lax.broadcasted_iota(I32, x.shape, 1)
    first = (lane % (2 * half)) < half
    swapped = jnp.where(first, pltpu.roll(x, w - half, 1), pltpu.roll(x, half, 1))
    return x * cos + swapped * sin


def _prep_kernel(rq_ref, rk_ref, dq_ref, dk_ref, nq_ref, nk_ref, gains_ref, gmat_ref, tab_ref,
                 orq_ref, ork_ref, odq_ref, odk_ref, onq_ref, onk_ref):
    reps = SEG // 128
    cos_r = jnp.concatenate([tab_ref[0]] * reps, axis=1)
    sin_r = jnp.concatenate([tab_ref[1]] * reps, axis=1)
    cos_d = jnp.concatenate([tab_ref[2]] * reps, axis=1)
    sin_d = jnp.concatenate([tab_ref[3]] * reps, axis=1)
    gmat = gmat_ref[...]

    def normed(ref, row):
        x = ref[...].astype(F32)
        return x * lax.rsqrt(_group_mean_sq(x, gmat) + EPS) * gains_ref[row:row + 1, :]

    orq_ref[...] = _rope(rq_ref[...].astype(F32), cos_r, sin_r, RET_DK // 4).astype(BF16)
    ork_ref[...] = _rope(rk_ref[...].astype(F32) * (RET_DK ** -0.5), cos_r, sin_r, RET_DK // 4).astype(BF16)
    odq_ref[...] = (_rope(normed(dq_ref, 0), cos_d, sin_d, DIFF_HD // 4) * (DIFF_HD ** -0.5)).astype(BF16)
    odk_ref[...] = _rope(normed(dk_ref, 1), cos_d, sin_d, DIFF_HD // 4).astype(BF16)
    onq_ref[...] = (normed(nq_ref, 2) * (NA_HD ** -0.5)).astype(BF16)
    onk_ref[...] = normed(nk_ref, 3).astype(BF16)


def _prep(z, gains, gmat, tables, dims):
    R = z.shape[0]
    tm = _tile(dims["BC"], 512)
    nlt, tpb = dims["BS"] // tm, dims["S"] // tm
    zspec = lambda c: pl.BlockSpec((tm, SEG), lambda i: (i, c))
    tab = lambda i: (0, jnp.where(i < nlt, i % tpb, tpb), 0)
    ospec = pl.BlockSpec((tm, SEG), lambda i: (i, 0))
    return pl.pallas_call(
        _prep_kernel,
        out_shape=[jax.ShapeDtypeStruct((R, SEG), BF16)] * 6,
        grid=(R // tm,),
        in_specs=[zspec(C_RQ), zspec(C_RK), zspec(C_DQ), zspec(C_DK), zspec(C_NQ), zspec(C_NK),
                  pl.BlockSpec((8, SEG), lambda i: (0, 0)),
                  pl.BlockSpec((SEG, SEG), lambda i: (0, 0)),
                  pl.BlockSpec((4, tm, 128), tab)],
        out_specs=[ospec] * 6,
        compiler_params=_cparams(("parallel",)),
        name="prep",
    )(z, z, z, z, z, z, gains, gmat, tables)


def _rope_tables(S, tm):
    t = jnp.arange(S)
    row, col = t // GRID_W, t % GRID_W

    def one(d, width):
        inv = jnp.power(ROPE_BASE, -jnp.arange(0, d, 2, dtype=F32) / d)
        ar = row.astype(F32)[:, None] * inv[None, :]
        ac = col.astype(F32)[:, None] * inv[None, :]
        cos = jnp.concatenate([jnp.cos(ar), jnp.cos(ar), jnp.cos(ac), jnp.cos(ac)], axis=-1)
        sin = jnp.concatenate([-jnp.sin(ar), jnp.sin(ar), -jnp.sin(ac), jnp.sin(ac)], axis=-1)
        reps = width // cos.shape[-1]
        return jnp.tile(cos, (1, reps)), jnp.tile(sin, (1, reps))

    cr, sr = one(RET_DK // 2, 128)
    cd, sd = one(DIFF_HD // 2, 128)
    tab = jnp.stack([cr, sr, cd, sd])
    ident = jnp.stack([jnp.ones((tm, 128), F32), jnp.zeros((tm, 128), F32)] * 2)
    return jnp.concatenate([tab, ident], axis=1)


def _ret_kernel(lg_ref, qf_ref, kf_ref, vf_ref, qb_ref, kb_ref, vb_ref, of_ref, ob_ref, s_ref):
    C = RET_CHUNK

    @pl.when(pl.program_id(1) == 0)
    def _():
        s_ref[...] = jnp.zeros_like(s_ref)

    ii = lax.broadcasted_iota(I32, (C, C), 0).astype(F32)
    jj = lax.broadcasted_iota(I32, (C, C), 1).astype(F32)
    ri = lax.broadcasted_iota(I32, (C, RET_DK), 0).astype(F32)

    for h in range(RET_HEADS):
        qs = slice(h * RET_DK, (h + 1) * RET_DK)
        vs = slice(h * RET_DV, (h + 1) * RET_DV)
        for d, (q_ref, k_ref, v_ref, o_ref) in enumerate(((qf_ref, kf_ref, vf_ref, of_ref),
                                                         (qb_ref, kb_ref, vb_ref, ob_ref))):
            lg = lg_ref[d, h]
            q, k, v = q_ref[:, qs], k_ref[:, qs], v_ref[:, vs]
            if d == 0:
                dist = ii - jj
                q_dec = jnp.exp(lg * (ri + 1.0))
                k_dec = jnp.exp(lg * (C - 1.0 - ri))
            else:
                dist = jj - ii
                q_dec = jnp.exp(lg * (C - ri))
                k_dec = jnp.exp(lg * ri)
            decay = jnp.where(dist >= 0, jnp.exp(lg * jnp.maximum(dist, 0.0)), 0.0)
            att = (_dot_nt(q, k) * decay).astype(BF16)
            s = s_ref[d, h]
            o = _dot(att, v) + _dot((q.astype(F32) * q_dec).astype(BF16), s.astype(BF16))
            s_ref[d, h] = s * jnp.exp(lg * C) + _dot_tn((k.astype(F32) * k_dec).astype(BF16), v)
            o_ref[:, vs] = o.astype(o_ref.dtype)


def _retention(lg, rq, rk, z, dims):
    B, S, C, BS = dims["B"], dims["S"], dims["C"], dims["BS"]
    R = rq.shape[0]
    nc, nl = C // RET_CHUNK, S // RET_CHUNK
    base = BS // RET_CHUNK
    fwd = lambda b, c: jnp.where(c < nc, base + b * nc + c, b * nl + (c - nc))
    bwd = lambda b, c: jnp.where(c < nc, base + b * nc + (nc - 1 - c), b * nl + (nl - 1 - (c - nc)))
    vw = RET_HEADS * RET_DV
    spec = lambda w, col, f: pl.BlockSpec((RET_CHUNK, w), lambda b, c: (f(b, c), col))
    return pl.pallas_call(
        _ret_kernel,
        out_shape=[jax.ShapeDtypeStruct((R, vw), BF16), jax.ShapeDtypeStruct((R, vw), BF16)],
        grid=(B, nc + nl),
        in_specs=[pl.BlockSpec(memory_space=pltpu.SMEM),
                  spec(SEG, 0, fwd), spec(SEG, 0, fwd), spec(vw, C_RV * SEG // vw, fwd),
                  spec(SEG, 0, bwd), spec(SEG, 0, bwd), spec(vw, C_RV * SEG // vw, bwd)],
        out_specs=[spec(vw, 0, fwd), spec(vw, 0, bwd)],
        scratch_shapes=[pltpu.VMEM((2, RET_HEADS, RET_DK, RET_DV), F32)],
        compiler_params=_cparams(("parallel", "arbitrary")),
        name="retention",
    )(lg, rq, rk, z, rq, rk, z)


def _split_halves(q):
    lane = lax.broadcasted_iota(I32, q.shape, 1)
    lo = jnp.where(lane < 64, q, jnp.zeros_like(q))
    hi = jnp.where(lane >= 64, q, jnp.zeros_like(q))
    return jnp.concatenate([lo, hi], axis=0)


def _fold_lanes(x, op):
    acc = x[:, :128]
    for j in range(1, x.shape[1] // 128):
        acc = op(acc, x[:, j * 128:(j + 1) * 128])
    return acc


def _attend(q2, sources, lanes, s_ref, e_ref):
    chunks = []
    off = 0
    for k_ref, v_ref, row0, n_rows, bias_fn in sources:
        for c0 in range(0, n_rows, KEY_CHUNK):
            n = min(KEY_CHUNK, n_rows - c0)
            chunks.append((k_ref, v_ref, row0 + c0, n, off, c0, bias_fn))
            off += n
    aligned = lambda r: r if isinstance(r, int) else pl.multiple_of(r, 128)
    accs, sums = [], []
    for b, rb in enumerate(range(0, q2.shape[0], ROW_BLOCK)):
        q = q2[rb:rb + ROW_BLOCK]
        buf = b % 2
        m = None
        for k_ref, _, r0, n, off, c0, bias_fn in chunks:
            s = _dot_nt(q, k_ref[pl.ds(aligned(r0), n), lanes])
            if bias_fn is not None:
                s = s + bias_fn(rb, c0, n)
            s_ref[buf, :, off:off + n] = s
            blk = _fold_lanes(s, jnp.maximum)
            m = blk if m is None else jnp.maximum(m, blk)
        m = m.max(axis=-1, keepdims=True)
        l = jnp.zeros((ROW_BLOCK, 128), F32)
        for _, _, _, n, off, _, _ in chunks:
            e = jnp.exp(s_ref[buf, :, off:off + n] - m)
            l = l + _fold_lanes(e, jnp.add)
            e_ref[buf, :, off:off + n] = e.astype(BF16)
        acc, off = None, 0
        for _, v_ref, row0, n_rows, _ in sources:
            t = _dot(e_ref[buf, :, off:off + n_rows], v_ref[pl.ds(aligned(row0), n_rows), lanes])
            acc = t if acc is None else acc + t
            off += n_rows
        accs.append(acc)
        sums.append(l.sum(axis=-1, keepdims=True))
    return jnp.concatenate(accs, axis=0), jnp.concatenate(sums, axis=0)


def _diff_kernel(lam_ref, q_ref, *refs, has_latent):
    if has_latent:
        kl_ref, vl_ref, kc_ref, vc_ref, o_ref, s_ref, e_ref = refs
        sources = ((kl_ref, vl_ref, 0, kl_ref.shape[0], None), (kc_ref, vc_ref, 0, kc_ref.shape[0], None))
    else:
        kc_ref, vc_ref, o_ref, s_ref, e_ref = refs
        sources = ((kc_ref, vc_ref, 0, kc_ref.shape[0], None),)
    lam = lam_ref[0, 0]
    tq = q_ref.shape[0]
    for h in range(DIFF_HEADS):
        hs = slice(h * 128, (h + 1) * 128)
        acc, l = _attend(_split_halves(q_ref[:, hs]), sources, hs, s_ref, e_ref)
        o = acc[:tq] * (1.0 / l[:tq]) - acc[tq:] * (lam / l[tq:])
        o_ref[:, hs] = o.astype(o_ref.dtype)


def _diff_attention(lam, dq, dk, z, dims, latent):
    B, S, C, BS = dims["B"], dims["S"], dims["C"], dims["BS"]
    lat_k = pl.BlockSpec((S, SEG), lambda b, i: (b, 0))
    lat_v = pl.BlockSpec((S, SEG), lambda b, i: (b, C_DV))
    ctx_k = pl.BlockSpec((C, SEG), lambda b, i: (BS // C + b, 0))
    ctx_v = pl.BlockSpec((C, SEG), lambda b, i: (BS // C + b, C_DV))
    if latent:
        tq = _tile(S, 256)
        nq = S // tq
        qmap = omap = lambda b, i: (b * nq + i, 0)
        in_specs = [lat_k, lat_v, ctx_k, ctx_v]
        args = (dk, z, dk, z)
        n_keys = S + C
    else:
        tq, nq = C, 1
        qmap = lambda b, i: (BS // C + b, 0)
        omap = lambda b, i: (b, 0)
        in_specs = [ctx_k, ctx_v]
        args = (dk, z)
        n_keys = C
    return pl.pallas_call(
        functools.partial(_diff_kernel, has_latent=latent),
        out_shape=jax.ShapeDtypeStruct((B * nq * tq, SEG), BF16),
        grid=(B, nq),
        in_specs=[pl.BlockSpec(memory_space=pltpu.SMEM), pl.BlockSpec((tq, SEG), qmap)] + in_specs,
        out_specs=pl.BlockSpec((tq, SEG), omap),
        scratch_shapes=[pltpu.VMEM((2, ROW_BLOCK, n_keys), F32), pltpu.VMEM((2, ROW_BLOCK, n_keys), BF16)],
        compiler_params=_cparams(("parallel", "arbitrary")),
        name="diff_lat" if latent else "diff_ctx",
    )(lam, dq, *args)


def _na_kernel(q_ref, *refs, has_local, rows):
    if has_local:
        kl_ref, vl_ref, kc_ref, vc_ref, tidx_ref, tile_ref, o_ref = refs
        qb = pl.program_id(1)
        kr0 = jnp.clip(qb * NA_QROWS - (NA_KROWS - NA_QROWS) // 2, 0, rows - NA_KROWS)
        k0 = pl.multiple_of(kr0 * GRID_W, 256)
        nk = NA_KROWS * GRID_W
        last = rows // NA_QROWS - 1
        btype = jnp.where(qb == 0, 0, jnp.where(qb == last, 2, 1))

        def head_bias(h):
            row_blocks = []
            for i in range(NA_QROWS):
                base = (btype * NA_QROWS + i) * (NA_KROWS // 2)
                row_blocks.append(jnp.concatenate(
                    [tile_ref[0, h, tidx_ref[base + jp]] for jp in range(NA_KROWS // 2)], axis=1))
            return jnp.concatenate(row_blocks, axis=0)
    else:
        kc_ref, vc_ref, o_ref = refs
    tq = q_ref.shape[0]
    lane = lax.broadcasted_iota(I32, (tq, 128), 1)
    for p in range(NA_HEADS // 2):
        ps = slice(p * 128, (p + 1) * 128)
        q2 = _split_halves(q_ref[:, ps])
        parts = [_dot_nt(q2, kc_ref[:, ps])]
        vals = [vc_ref[:, ps]]
        if has_local:
            s_loc = _dot_nt(q2, kl_ref[pl.ds(k0, nk), ps])
            bias = jnp.concatenate([head_bias(2 * p), head_bias(2 * p + 1)], axis=0).astype(F32)
            parts.append(s_loc + bias)
            vals.append(vl_ref[pl.ds(k0, nk), ps])
        m = parts[0].max(axis=-1, keepdims=True)
        for s in parts[1:]:
            m = jnp.maximum(m, s.max(axis=-1, keepdims=True))
        es = [jnp.exp(s - m) for s in parts]
        den = es[0].sum(axis=-1, keepdims=True)
        for e in es[1:]:
            den = den + e.sum(axis=-1, keepdims=True)
        rden = 1.0 / den
        o = None
        for e, v in zip(es, vals):
            t = _dot((e * rden).astype(BF16), v)
            o = t if o is None else o + t
        o_ref[:, ps] = jnp.where(lane < 64, o[:tq], o[tq:]).astype(o_ref.dtype)


def _na_attention(nq_, nk_, z, bias, layer, dims, latent):
    B, S, C, BS = dims["B"], dims["S"], dims["C"], dims["BS"]
    ctx_k = pl.BlockSpec((C, SEG), lambda b, i: (BS // C + b, 0))
    ctx_v = pl.BlockSpec((C, SEG), lambda b, i: (BS // C + b, C_NV))
    if latent:
        tq = NA_QROWS * GRID_W
        nq = S // tq
        qmap = omap = lambda b, i: (b * nq + i, 0)
        tiles, tidx = bias
        in_specs = [pl.BlockSpec((S, SEG), lambda b, i: (b, 0)),
                    pl.BlockSpec((S, SEG), lambda b, i: (b, C_NV)),
                    ctx_k, ctx_v,
                    pl.BlockSpec(memory_space=pltpu.SMEM),
                    pl.BlockSpec((1,) + tiles.shape[1:], lambda b, i: (layer, 0, 0, 0, 0))]
        args = (nk_, z, nk_, z, tidx, tiles)
    else:
        tq, nq = C, 1
        qmap = lambda b, i: (BS // C + b, 0)
        omap = lambda b, i: (b, 0)
        in_specs = [ctx_k, ctx_v]
        args = (nk_, z)
    return pl.pallas_call(
        functools.partial(_na_kernel, has_local=latent, rows=S // GRID_W),
        out_shape=jax.ShapeDtypeStruct((B * nq * tq, SEG), BF16),
        grid=(B, nq),
        in_specs=[pl.BlockSpec((tq, SEG), qmap)] + in_specs,
        out_specs=pl.BlockSpec((tq, SEG), omap),
        compiler_params=_cparams(("parallel", "arbitrary")),
        name="na_lat" if latent else "na_ctx",
    )(nq_, *args)


def _na_bias_tables(rpb, S):
    rows = S // GRID_W
    nqb = rows // NA_QROWS
    n_off = 2 * NA_KR - 1
    i = np.arange(NA_QROWS)[:, None]
    j = np.arange(NA_KROWS)[None, :]
    code = np.zeros((3, NA_QROWS, NA_KROWS), np.int64)
    for t, qb in enumerate((0, min(1, nqb - 1), nqb - 1)):
        r = qb * NA_QROWS + i
        kr0 = np.clip(qb * NA_QROWS - (NA_KROWS - NA_QROWS) // 2, 0, rows - NA_KROWS)
        kr = kr0 + j
        r0 = np.clip(r - NA_KR // 2, 0, rows - NA_KR)
        ok = (kr >= r0) & (kr < r0 + NA_KR)
        code[t] = np.where(ok, kr - r + NA_KR - 1, n_off)
    pairs = code.reshape(3, NA_QROWS, NA_KROWS // 2, 2)
    uniq, inverse = np.unique(pairs.reshape(-1, 2), axis=0, return_inverse=True)
    cq = np.arange(GRID_W)[:, None]
    ck = np.arange(GRID_W)[None, :]
    c0 = np.clip(cq - NA_KW // 2, 0, GRID_W - NA_KW)
    okc = (ck >= c0) & (ck < c0 + NA_KW)
    offc = np.clip(ck - cq + NA_KW - 1, 0, 2 * NA_KW - 2)
    sel_c = np.zeros((GRID_W, GRID_W, 2 * NA_KW - 1), np.float32)
    sel_c[np.broadcast_to(cq, okc.shape)[okc], np.broadcast_to(ck, okc.shape)[okc], offc[okc]] = 1.0
    planes = jnp.einsum("lhab,qkb->lhaqk", rpb.astype(F32), jnp.asarray(sel_c), precision=lax.Precision.HIGHEST)
    planes = jnp.where(jnp.asarray(okc)[None, None, None], planes, NEG)
    planes = jnp.concatenate([planes, jnp.full_like(planes[:, :, :1], NEG)], axis=2)
    tiles = jnp.concatenate([planes[:, :, uniq[:, 0]], planes[:, :, uniq[:, 1]]], axis=-1)
    return tiles.astype(BF16), jnp.asarray(inverse.reshape(-1), I32)


def _head_norm(x, width, gain):
    outs = []
    for h in range(x.shape[-1] // width):
        xh = x[:, h * width:(h + 1) * width]
        outs.append(xh * lax.rsqrt(jnp.mean(xh * xh, axis=-1, keepdims=True) + EPS))
    return jnp.concatenate(outs, axis=-1) * gain


def _merge_kernel(x_ref, gate_ref, rf_ref, rb_ref, rg_ref, *refs, diff_scale, n_lat_tiles):
    if n_lat_tiles is None:
        od_ref, on_ref = refs[:2]
        od, on = od_ref[...], on_ref[...]
    else:
        odl_ref, onl_ref, odc_ref, onc_ref = refs[:4]
        is_lat = pl.program_id(0) < n_lat_tiles
        od = jnp.where(is_lat, odl_ref[...], odc_ref[...])
        on = jnp.where(is_lat, onl_ref[...], onc_ref[...])
    g1_ref, g2_ref, g3_ref, retg_ref, dg_ref, wr_ref, wd_ref, wn_ref, wo_ref, o_ref = refs[-10:]
    r = _head_norm(rf_ref[...].astype(F32) + rb_ref[...].astype(F32), RET_DV, retg_ref[...])
    rg = rg_ref[...].astype(F32)
    y_ret = _dot((rg * _sigmoid(rg) * r).astype(BF16), wr_ref[0])
    d = _head_norm(od.astype(F32), DIFF_DV, dg_ref[...]) * diff_scale
    y_diff = _dot(d.astype(BF16), wd_ref[0])
    y_na = _dot(on, wn_ref[0])
    m = (_sigmoid(g1_ref[...].astype(F32)) * y_ret + _sigmoid(g2_ref[...].astype(F32)) * y_diff
         + _sigmoid(g3_ref[...].astype(F32)) * y_na)
    y = _dot(m.astype(BF16), wo_ref[0])
    o_ref[...] = x_ref[...] + gate_ref[0] * y


def _merge(x, gate, rf, rb, z, attn, ret_g, d_g, wr, wd, wn, wo, layer, diff_scale, dims, rows):
    D = x.shape[1]
    tm = _tile(dims["BC"], 256)
    nlt, tpb, B = dims["BS"] // tm, dims["S"] // tm, dims["B"]
    rw = RET_HEADS * RET_DV
    gcol = C_GATES * SEG // D
    row = lambda w, col=0: pl.BlockSpec((tm, w), lambda i: (i, col))
    lat = pl.BlockSpec((tm, SEG), lambda i: (jnp.minimum(i, nlt - 1), 0))
    ctx = pl.BlockSpec((tm, SEG), lambda i: (jnp.maximum(i - nlt, 0), 0))
    const = lambda a: pl.BlockSpec(a.shape, lambda i: (0,) * a.ndim)
    wspec = lambda a: pl.BlockSpec((1,) + a.shape[1:], lambda i: (layer, 0, 0))
    with_ctx = len(attn) == 4
    return pl.pallas_call(
        functools.partial(_merge_kernel, diff_scale=diff_scale, n_lat_tiles=nlt if with_ctx else None),
        out_shape=jax.ShapeDtypeStruct((rows, D), F32),
        grid=(rows // tm,),
        in_specs=[row(D), pl.BlockSpec((1, 1, D), lambda i: (_mod_index(i, nlt, tpb, B), 0, 0)),
                  row(rw), row(rw), row(rw, C_RG * SEG // rw)]
                 + ([lat, lat, ctx, ctx] if with_ctx else [row(SEG), row(SEG)])
                 + [row(D, gcol), row(D, gcol + 1), row(D, gcol + 2),
                    const(ret_g), const(d_g), wspec(wr), wspec(wd), wspec(wn), wspec(wo)],
        out_specs=row(D),
        compiler_params=_cparams(("parallel",)),
        name="merge",
    )(x, gate, rf, rb, z, *attn, z, z, z, ret_g, d_g, wr, wd, wn, wo)


def _to_token_tiles(ref, x):
    rows, width = x.shape
    n = width // 128
    for c in range(n):
        ref[pl.ds(c, rows, stride=n), :] = x[:, c * 128:(c + 1) * 128]


def _from_token_tiles(ref, rows):
    n = ref.shape[0] // rows
    return jnp.concatenate([ref[pl.ds(c, rows, stride=n), :] for c in range(n)], axis=1)


def _router_kernel(x_ref, g_ref, sh_ref, sc_ref, wrt_ref, br_ref, h_ref, idx_ref, wts_ref, cnt_ref, run_ref):
    i = pl.program_id(0)
    tm = x_ref.shape[0]
    E = N_EXPERTS

    @pl.when(i == 0)
    def _():
        run_ref[...] = jnp.zeros_like(run_ref)

    h = _modnorm(x_ref[...], g_ref[...], sh_ref[0], sc_ref[0])
    _to_token_tiles(h_ref, h)
    h_hi = h.astype(BF16)
    h_lo = (h - h_hi.astype(F32)).astype(BF16)
    w = wrt_ref[...]
    w_hi = w.astype(BF16)
    w_lo = (w - w_hi.astype(F32)).astype(BF16)
    logits = _dot_nt(w_hi, h_hi) + _dot_nt(w_hi, h_lo) + _dot_nt(w_lo, h_hi)
    s = _sigmoid(logits)
    sel = s + br_ref[...]

    best, gi = None, None
    for g in range(N_GROUPS):
        r = [sel[g * EXPERTS_PER_GROUP + k:g * EXPERTS_PER_GROUP + k + 1] for k in range(EXPERTS_PER_GROUP)]
        top2 = None
        for a in range(EXPERTS_PER_GROUP):
            for b in range(a + 1, EXPERTS_PER_GROUP):
                pair = r[a] + r[b]
                top2 = pair if top2 is None else jnp.maximum(top2, pair)
        if best is None:
            best, gi = top2, jnp.zeros(top2.shape, I32)
        else:
            upd = top2 > best
            gi = jnp.where(upd, g, gi)
            best = jnp.where(upd, top2, best)

    eid = lax.broadcasted_iota(I32, (E, tm), 0)
    masked = jnp.where(eid // EXPERTS_PER_GROUP == gi, sel, -jnp.inf)
    m1 = masked.max(axis=0, keepdims=True)
    i1 = jnp.where(masked == m1, eid, E).min(axis=0, keepdims=True)
    masked2 = jnp.where(eid == i1, -jnp.inf, masked)
    m2 = masked2.max(axis=0, keepdims=True)
    i2 = jnp.where(masked2 == m2, eid, E).min(axis=0, keepdims=True)
    oh1 = eid == i1
    oh2 = eid == i2
    w1 = jnp.where(oh1, s, 0.0).sum(axis=0, keepdims=True)
    w2 = jnp.where(oh2, s, 0.0).sum(axis=0, keepdims=True)
    tot = w1 + w2

    oh = (oh1 | oh2).astype(F32)
    before = (lax.broadcasted_iota(I32, (tm, tm), 0) < lax.broadcasted_iota(I32, (tm, tm), 1)).astype(BF16)
    prefix = _dot(oh.astype(BF16), before) + run_ref[:, 0:1]
    rank1 = jnp.where(oh1, prefix, 0.0).sum(axis=0, keepdims=True)
    rank2 = jnp.where(oh2, prefix, 0.0).sum(axis=0, keepdims=True)
    run_ref[...] = run_ref[...] + oh.sum(axis=1, keepdims=True)

    idx_ref[...] = jnp.zeros_like(idx_ref)
    wts_ref[...] = jnp.zeros_like(wts_ref)
    for k, v in enumerate((i1, i2, rank1.astype(I32), rank2.astype(I32))):
        idx_ref[0, k:k + 1, :] = v
    wts_ref[0, 0:1, :] = w1 / tot
    wts_ref[0, 1:2, :] = w2 / tot
    cnt_ref[...] = run_ref[...]


def _router(x, g, shift, scale, wrt, br, dims, rows):
    D = x.shape[1]
    tm = _tile(dims["BC"], 256)
    nt = rows // tm
    nlt, tpb, B = dims["BS"] // tm, dims["S"] // tm, dims["B"]
    mod = lambda i: (_mod_index(i, nlt, tpb, B), 0, 0)
    n = D // 128
    return pl.pallas_call(
        _router_kernel,
        out_shape=[jax.ShapeDtypeStruct((rows * n, 128), F32),
                   jax.ShapeDtypeStruct((nt, 8, tm), I32),
                   jax.ShapeDtypeStruct((nt, 8, tm), F32),
                   jax.ShapeDtypeStruct((N_EXPERTS, 128), F32)],
        grid=(nt,),
        in_specs=[pl.BlockSpec((tm, D), lambda i: (i, 0)),
                  pl.BlockSpec((1, D), lambda i: (0, 0)),
                  pl.BlockSpec((1, 1, D), mod), pl.BlockSpec((1, 1, D), mod),
                  pl.BlockSpec((N_EXPERTS, D), lambda i: (0, 0)),
                  pl.BlockSpec((N_EXPERTS, 1), lambda i: (0, 0))],
        out_specs=[pl.BlockSpec((tm * n, 128), lambda i: (i, 0)),
                   pl.BlockSpec((1, 8, tm), lambda i: (i, 0, 0)),
                   pl.BlockSpec((1, 8, tm), lambda i: (i, 0, 0)),
                   pl.BlockSpec((N_EXPERTS, 128), lambda i: (0, 0))],
        scratch_shapes=[pltpu.VMEM((N_EXPERTS, 128), F32)],
        compiler_params=_cparams(("arbitrary",)),
        name="router",
    )(x, g, shift, scale, wrt, br)


GATHER_UNROLL = 8


def _token_copy(r, n, idx_ref, src_hbm, dst_ref, sem):
    src = pl.multiple_of(idx_ref[0, 0, r] * n, n)
    dst = pl.multiple_of(r * n, n)
    return pltpu.make_async_copy(src_hbm.at[pl.ds(src, n)], dst_ref.at[pl.ds(dst, n)], sem)


def _start_gather(n_tok, n, idx_ref, src_hbm, dst_ref, sem):
    def body(g, carry):
        for u in range(GATHER_UNROLL):
            _token_copy(g * GATHER_UNROLL + u, n, idx_ref, src_hbm, dst_ref, sem).start(priority=u % 2)
        return carry

    lax.fori_loop(0, n_tok // GATHER_UNROLL, body, 0)


def _wait_gather(n_tok, n, idx_ref, src_hbm, dst_ref, sem):
    def body(g, carry):
        for u in range(GATHER_UNROLL):
            _token_copy(g * GATHER_UNROLL + u, n, idx_ref, src_hbm, dst_ref, sem).wait()
        return carry

    lax.fori_loop(0, n_tok // GATHER_UNROLL, body, 0)


def _pipelined_gathers(streams, n_tok, n):
    t = pl.program_id(0)
    last = pl.num_programs(0) - 1
    slot = lax.rem(t, 2)

    @pl.when(t == 0)
    def _():
        for cur, _, src, buf, sem in streams:
            _start_gather(n_tok, n, cur, src, buf.at[0], sem.at[0])

    @pl.when(t < last)
    def _():
        for _, nxt, src, buf, sem in streams:
            _start_gather(n_tok, n, nxt, src, buf.at[1 - slot], sem.at[1 - slot])

    for cur, _, src, buf, sem in streams:
        _wait_gather(n_tok, n, cur, src, buf.at[slot], sem.at[slot])
    return slot


def _expert_kernel(te_ref, cur_ref, nxt_ref, h_hbm, wg_ref, wu_ref, wd_ref, o_ref, x_buf, sem, *, rows, n):
    slot = _pipelined_gathers(((cur_ref, nxt_ref, h_hbm, x_buf, sem),), rows, n)
    x = _from_token_tiles(x_buf.at[slot], rows).astype(BF16)
    gate = _dot(x, wg_ref[0, 0])
    a = gate * _sigmoid(gate) * _dot(x, wu_ref[0, 0])
    _to_token_tiles(o_ref, _dot(a.astype(BF16), wd_ref[0, 0]))


def _experts(tile_expert, src_rows, h, wg, wu, wd, layer, tmE):
    D, De = wg.shape[2:]
    n = D // 128
    P = src_rows.shape[0]
    nt = P // tmE
    wspec = lambda a, b: pl.BlockSpec((1, 1, a, b), lambda t, te: (layer, te[t], 0, 0))
    idx = src_rows.reshape(nt, 1, tmE)
    cur = pl.BlockSpec((1, 1, tmE), lambda t, te: (t, 0, 0), memory_space=pltpu.SMEM)
    nxt = pl.BlockSpec((1, 1, tmE), lambda t, te: (jnp.minimum(t + 1, nt - 1), 0, 0), memory_space=pltpu.SMEM)
    return pl.pallas_call(
        functools.partial(_expert_kernel, rows=tmE, n=n),
        out_shape=jax.ShapeDtypeStruct((P * n, 128), F32),
        grid_spec=pltpu.PrefetchScalarGridSpec(
            num_scalar_prefetch=1,
            grid=(nt,),
            in_specs=[cur, nxt, pl.BlockSpec(memory_space=pl.ANY), wspec(D, De), wspec(D, De), wspec(De, D)],
            out_specs=pl.BlockSpec((tmE * n, 128), lambda t, te: (t, 0)),
            scratch_shapes=[pltpu.VMEM((2, tmE * n, 128), F32), pltpu.SemaphoreType.DMA((2,))]),
        compiler_params=_cparams(("arbitrary",)),
        name="experts",
    )(tile_expert, idx, idx, h, wg, wu, wd)


def _combine_kernel(c1_ref, n1_ref, c2_ref, n2_ref, x_ref, gate_ref, w_ref, ye_hbm, o_ref,
                    b1_ref, b2_ref, sem1, sem2):
    rows, width = x_ref.shape
    n = width // 128
    slot = _pipelined_gathers(((c1_ref, n1_ref, ye_hbm, b1_ref, sem1), (c2_ref, n2_ref, ye_hbm, b2_ref, sem2)),
                              rows, n)
    w = w_ref[...]
    y = (w[:, 0:1] * _from_token_tiles(b1_ref.at[slot], rows)
         + w[:, 1:2] * _from_token_tiles(b2_ref.at[slot], rows))
    o_ref[...] = x_ref[...] + gate_ref[0] * y


def _combine(pos1, pos2, x, gate, w, ye, dims, rows):
    D = x.shape[1]
    n = D // 128
    tc = _tile(dims["BC"], 256)
    nt = rows // tc
    nlt, tpb, B = dims["BS"] // tc, dims["S"] // tc, dims["B"]
    cur = pl.BlockSpec((1, 1, tc), lambda i: (i, 0, 0), memory_space=pltpu.SMEM)
    nxt = pl.BlockSpec((1, 1, tc), lambda i: (jnp.minimum(i + 1, nt - 1), 0, 0), memory_space=pltpu.SMEM)
    p1, p2 = pos1.reshape(nt, 1, tc), pos2.reshape(nt, 1, tc)
    return pl.pallas_call(
        _combine_kernel,
        out_shape=jax.ShapeDtypeStruct((rows, D), F32),
        grid=(nt,),
        in_specs=[cur, nxt, cur, nxt,
                  pl.BlockSpec((tc, D), lambda i: (i, 0)),
                  pl.BlockSpec((1, 1, D), lambda i: (_mod_index(i, nlt, tpb, B), 0, 0)),
                  pl.BlockSpec((tc, 2), lambda i: (i, 0)),
                  pl.BlockSpec(memory_space=pl.ANY)],
        out_specs=pl.BlockSpec((tc, D), lambda i: (i, 0)),
        scratch_shapes=[pltpu.VMEM((2, tc * n, 128), F32), pltpu.VMEM((2, tc * n, 128), F32),
                        pltpu.SemaphoreType.DMA((2,)), pltpu.SemaphoreType.DMA((2,))],
        compiler_params=_cparams(("arbitrary",)),
        name="combine",
    )(p1, p1, p2, p2, x, gate, w, ye)


def _moe(x, g, shift, scale, gate, wrt, br, wg, wu, wd, layer, dims, rows):
    tmE = 256
    h, idx, wts, cnt = _router(x, g, shift, scale, wrt, br, dims, rows)
    nt = idx.shape[0]
    flat = lambda a, k: a[:, k, :].reshape(rows)
    e1, e2, r1, r2 = (flat(idx, k) for k in range(4))
    counts = cnt[:, 0].astype(I32)
    padded = (counts + tmE - 1) // tmE * tmE
    ends = jnp.cumsum(padded)
    starts = ends - padded
    pos1 = starts[e1] + r1
    pos2 = starts[e2] + r2
    P = (2 * rows + N_EXPERTS * tmE) // tmE * tmE
    tok = jnp.arange(rows, dtype=I32)
    src = jnp.zeros((P,), I32).at[jnp.concatenate([pos1, pos2])].set(
        jnp.concatenate([tok, tok]), unique_indices=True, mode="promise_in_bounds")
    tile_start = jnp.arange(P // tmE, dtype=I32) * tmE
    tile_expert = jnp.minimum(jnp.sum(ends[None, :] <= tile_start[:, None], axis=1), N_EXPERTS - 1).astype(I32)
    ye = _experts(tile_expert, src, h, wg, wu, wd, layer, tmE)
    w = jnp.stack([flat(wts, 0), flat(wts, 1)], axis=-1)
    return _combine(pos1, pos2, x, gate, w, ye, dims, rows)


def kernel(x, c, ctx, c_ctx, w_ada, b_ada, norm1_g, norm2_g, w_in, ret_decay_logit, ret_norm_g, diff_q_norm_g, diff_k_norm_g, diff_lambda, diff_norm_g, na_q_norm_g, na_k_norm_g, na_rpb, w_ret_out, w_diff_out, w_na_out, w_out, w_router, b_router, w_exp_gate, w_exp_up, w_exp_down):
    B, S, D = x.shape
    C = ctx.shape[1]
    L = w_in.shape[0]
    BS, BC = B * S, B * C
    assert S % (NA_QROWS * GRID_W) == 0 and S // GRID_W >= NA_KROWS and C % RET_CHUNK == 0
    assert (C_GATES * SEG) % D == 0 and w_in.shape[2] == C_GATES * SEG + 3 * D
    dims = dict(B=B, S=S, C=C, BS=BS, BC=BC)

    n_c = -(-(B + 1) // 8) * 8
    c_all = jnp.concatenate([c, c_ctx[None], jnp.zeros((n_c - B - 1, D), F32)], axis=0)
    mods = _ada(c_all, w_ada, b_ada)[:, :B + 1]
    mod = lambda l, k: mods[l, :, k * D:(k + 1) * D].reshape(B + 1, 1, D)

    xa = jnp.concatenate([x.reshape(BS, D), ctx.reshape(BC, D)], axis=0)

    tm_prep = _tile(BC, 512)
    tables = _rope_tables(S, tm_prep)
    gmat = jnp.asarray(np.kron(np.eye(SEG // 64), np.full((64, 64), 1.0 / 64)), BF16)
    bias = _na_bias_tables(na_rpb, S)
    lg = jax.nn.log_sigmoid(ret_decay_logit.astype(F32))
    lam_p = diff_lambda.astype(F32)
    wrt = w_router.T
    br = b_router.astype(F32).reshape(N_EXPERTS, 1)
    tile8 = lambda v: jnp.tile(v.astype(F32), SEG // v.shape[0])
    row = lambda v: v.astype(F32).reshape(1, -1)
    w_in_b, w_ret_b, w_diff_b, w_na_b, w_out_b, w_gate_b, w_up_b, w_down_b = (
        w.astype(BF16) for w in (w_in, w_ret_out, w_diff_out, w_na_out, w_out, w_exp_gate, w_exp_up, w_exp_down))

    for l in range(L):
        last = l == L - 1
        lam_init = 0.8 - 0.6 * math.exp(-0.3 * l)
        lam = (jnp.exp(jnp.sum(lam_p[l, 0] * lam_p[l, 1])) - jnp.exp(jnp.sum(lam_p[l, 2] * lam_p[l, 3]))
               + lam_init).reshape(1, 1)
        z = _inproj(xa, row(norm1_g[l]), mod(l, 0), mod(l, 1), w_in_b, l, dims)
        gains = jnp.stack([tile8(diff_q_norm_g[l]), tile8(diff_k_norm_g[l]),
                           tile8(na_q_norm_g[l]), tile8(na_k_norm_g[l])] + [jnp.zeros((SEG,), F32)] * 4)
        rq, rk, dq, dk, nq, nk = _prep(z, gains, gmat, tables, dims)

        rf, rb = _retention(lg[l], rq, rk, z, dims)
        attn = (_diff_attention(lam, dq, dk, z, dims, True), _na_attention(nq, nk, z, bias, l, dims, True))
        rows = BS if last else BS + BC
        if not last:
            attn += (_diff_attention(lam, dq, dk, z, dims, False), _na_attention(nq, nk, z, None, l, dims, False))
        x1 = _merge(xa, mod(l, 2), rf, rb, z, attn, row(ret_norm_g[l]), row(diff_norm_g[l]),
                    w_ret_b, w_diff_b, w_na_b, w_out_b, l, 1.0 - lam_init, dims, rows)
        xa = _moe(x1, row(norm2_g[l]), mod(l, 3), mod(l, 4), mod(l, 5), wrt, br,
                  w_gate_b, w_up_b, w_down_b, l, dims, rows)
    return xa.reshape(B, S, D)
```

```python
import functools
import math

import numpy as np
import jax
import jax.numpy as jnp
from jax import lax
from jax.experimental import pallas as pl
from jax.experimental.pallas import tpu as pltpu

F32 = jnp.float32
BF16 = jnp.bfloat16
I32 = jnp.int32

GRID_W = 64
RET_HEADS, RET_DK, RET_DV, RET_CHUNK = 4, 128, 256, 128
DIFF_HEADS, DIFF_HD, DIFF_DV = 4, 64, 128
NA_HEADS, NA_HD, NA_KR, NA_KW = 8, 64, 8, 16
N_EXPERTS, N_GROUPS = 16, 4
EXPERTS_PER_GROUP = N_EXPERTS // N_GROUPS
ROPE_BASE = 10000.0
EPS = 1e-6
NEG = -1e30

SEG = 512
C_RQ, C_RK, C_RV, C_RG, C_DQ, C_DK, C_DV, C_NQ, C_NK, C_NV, C_GATES = 0, 1, 2, 4, 6, 7, 8, 9, 10, 11, 12
NA_QROWS = 8
NA_KROWS = 16
KEY_CHUNK = 256
ROW_BLOCK = 256

V7X_VMEM_BYTES = 64 * 1024 * 1024
VMEM_LIMIT = V7X_VMEM_BYTES - 8 * 1024 * 1024


def _cparams(sem):
    return pltpu.CompilerParams(dimension_semantics=sem, vmem_limit_bytes=VMEM_LIMIT)


def _sigmoid(x):
    return 1.0 / (1.0 + jnp.exp(-x))


def _tile(n, pref):
    t = min(n, pref)
    while n % t:
        t //= 2
    return t


def _dot_nt(a, b):
    return lax.dot_general(a, b, (((1,), (1,)), ((), ())), preferred_element_type=F32)


def _dot_tn(a, b):
    return lax.dot_general(a, b, (((0,), (0,)), ((), ())), preferred_element_type=F32)


def _dot(a, b):
    return jnp.dot(a, b, preferred_element_type=F32)


def _ada_kernel(c_ref, w_ref, b_ref, o_ref):
    c = c_ref[...]
    a = (c * _sigmoid(c)).astype(BF16)
    o_ref[0] = _dot(a, w_ref[0].astype(BF16)) + b_ref[0]


def _ada(c_all, w_ada, b_ada):
    L, D, N = w_ada.shape
    M = c_all.shape[0]
    tn = _tile(N, 1024)
    return pl.pallas_call(
        _ada_kernel,
        out_shape=jax.ShapeDtypeStruct((L, M, N), F32),
        grid=(L, N // tn),
        in_specs=[
            pl.BlockSpec((M, D), lambda l, j: (0, 0)),
            pl.BlockSpec((1, D, tn), lambda l, j: (l, 0, j)),
            pl.BlockSpec((1, 1, tn), lambda l, j: (l, 0, j)),
        ],
        out_specs=pl.BlockSpec((1, M, tn), lambda l, j: (l, 0, j)),
        compiler_params=_cparams(("parallel", "parallel")),
        name="ada",
    )(c_all, w_ada, b_ada.reshape(L, 1, N))


def _modnorm(x, g, shift, scale):
    ms = jnp.mean(x * x, axis=-1, keepdims=True)
    y = x * lax.rsqrt(ms + EPS) * g
    return y * (1.0 + scale) + shift


def _inproj_kernel(x_ref, g_ref, sh_ref, sc_ref, w_ref, o_ref, h_ref):
    @pl.when(pl.program_id(1) == 0)
    def _():
        h_ref[...] = _modnorm(x_ref[...], g_ref[...], sh_ref[0], sc_ref[0]).astype(BF16)

    o_ref[...] = _dot(h_ref[...], w_ref[0]).astype(o_ref.dtype)


def _mod_index(i, n_lat_tiles, tiles_per_batch, n_batch):
    return jnp.where(i < n_lat_tiles, i // tiles_per_batch, n_batch)


def _inproj(x, g, shift, scale, w, layer, dims):
    R, D = x.shape
    P = w.shape[2]
    tm = _tile(dims["BC"], 1024)
    tn = _tile(P, 1024)
    nlt, tpb, B = dims["BS"] // tm, dims["S"] // tm, dims["B"]
    mod = lambda i, j: (_mod_index(i, nlt, tpb, B), 0, 0)
    return pl.pallas_call(
        _inproj_kernel,
        out_shape=jax.ShapeDtypeStruct((R, P), BF16),
        grid=(R // tm, P // tn),
        in_specs=[
            pl.BlockSpec((tm, D), lambda i, j: (i, 0)),
            pl.BlockSpec((1, D), lambda i, j: (0, 0)),
            pl.BlockSpec((1, 1, D), mod),
            pl.BlockSpec((1, 1, D), mod),
            pl.BlockSpec((1, D, tn), lambda i, j: (layer, 0, j)),
        ],
        out_specs=pl.BlockSpec((tm, tn), lambda i, j: (i, j)),
        scratch_shapes=[pltpu.VMEM((tm, D), BF16)],
        compiler_params=_cparams(("parallel", "arbitrary")),
        name="inproj",
    )(x, g, shift, scale, w)


def _group_mean_sq(x, gmat):
    xx = x * x
    hi = xx.astype(BF16)
    lo = (xx - hi.astype(F32)).astype(BF16)
    return _dot(hi, gmat) + _dot(lo, gmat)


def _rope(x, cos, sin, half):
    w = x.shape[-1]
    lane = lax.broadcasted_iota(I32, x.shape, 1)
    first = (lane % (2 * half)) < half
    swapped = jnp.where(first, pltpu.roll(x, w - half, 1), pltpu.roll(x, half, 1))
    return x * cos + swapped * sin


def _prep_kernel(rq_ref, rk_ref, dq_ref, dk_ref, nq_ref, nk_ref, gains_ref, gmat_ref, tab_ref,
                 orq_ref, ork_ref, odq_ref, odk_ref, onq_ref, onk_ref):
    reps = SEG // 128
    cos_r = jnp.concatenate([tab_ref[0]] * reps, axis=1)
    sin_r = jnp.concatenate([tab_ref[1]] * reps, axis=1)
    cos_d = jnp.concatenate([tab_ref[2]] * reps, axis=1)
    sin_d = jnp.concatenate([tab_ref[3]] * reps, axis=1)
    gmat = gmat_ref[...]

    def normed(ref, row):
        x = ref[...].astype(F32)
        return x * lax.rsqrt(_group_mean_sq(x, gmat) + EPS) * gains_ref[row:row + 1, :]

    orq_ref[...] = _rope(rq_ref[...].astype(F32), cos_r, sin_r, RET_DK // 4).astype(BF16)
    ork_ref[...] = _rope(rk_ref[...].astype(F32) * (RET_DK ** -0.5), cos_r, sin_r, RET_DK // 4).astype(BF16)
    odq_ref[...] = (_rope(normed(dq_ref, 0), cos_d, sin_d, DIFF_HD // 4) * (DIFF_HD ** -0.5)).astype(BF16)
    odk_ref[...] = _rope(normed(dk_ref, 1), cos_d, sin_d, DIFF_HD // 4).astype(BF16)
    onq_ref[...] = (normed(nq_ref, 2) * (NA_HD ** -0.5)).astype(BF16)
    onk_ref[...] = normed(nk_ref, 3).astype(BF16)


def _prep(z, gains, gmat, tables, dims):
    R = z.shape[0]
    tm = _tile(dims["BC"], 512)
    nlt, tpb = dims["BS"] // tm, dims["S"] // tm
    zspec = lambda c: pl.BlockSpec((tm, SEG), lambda i: (i, c))
    tab = lambda i: (0, jnp.where(i < nlt, i % tpb, tpb), 0)
    ospec = pl.BlockSpec((tm, SEG), lambda i: (i, 0))
    return pl.pallas_call(
        _prep_kernel,
        out_shape=[jax.ShapeDtypeStruct((R, SEG), BF16)] * 6,
        grid=(R // tm,),
        in_specs=[zspec(C_RQ), zspec(C_RK), zspec(C_DQ), zspec(C_DK), zspec(C_NQ), zspec(C_NK),
                  pl.BlockSpec((8, SEG), lambda i: (0, 0)),
                  pl.BlockSpec((SEG, SEG), lambda i: (0, 0)),
                  pl.BlockSpec((4, tm, 128), tab)],
        out_specs=[ospec] * 6,
        compiler_params=_cparams(("parallel",)),
        name="prep",
    )(z, z, z, z, z, z, gains, gmat, tables)


def _rope_tables(S, tm):
    t = jnp.arange(S)
    row, col = t // GRID_W, t % GRID_W

    def one(d, width):
        inv = jnp.power(ROPE_BASE, -jnp.arange(0, d, 2, dtype=F32) / d)
        ar = row.astype(F32)[:, None] * inv[None, :]
        ac = col.astype(F32)[:, None] * inv[None, :]
        cos = jnp.concatenate([jnp.cos(ar), jnp.cos(ar), jnp.cos(ac), jnp.cos(ac)], axis=-1)
        sin = jnp.concatenate([-jnp.sin(ar), jnp.sin(ar), -jnp.sin(ac), jnp.sin(ac)], axis=-1)
        reps = width // cos.shape[-1]
        return jnp.tile(cos, (1, reps)), jnp.tile(sin, (1, reps))

    cr, sr = one(RET_DK // 2, 128)
    cd, sd = one(DIFF_HD // 2, 128)
    tab = jnp.stack([cr, sr, cd, sd])
    ident = jnp.stack([jnp.ones((tm, 128), F32), jnp.zeros((tm, 128), F32)] * 2)
    return jnp.concatenate([tab, ident], axis=1)


def _ret_kernel(lg_ref, qf_ref, kf_ref, vf_ref, qb_ref, kb_ref, vb_ref, of_ref, ob_ref, s_ref):
    C = RET_CHUNK

    @pl.when(pl.program_id(1) == 0)
    def _():
        s_ref[...] = jnp.zeros_like(s_ref)

    ii = lax.broadcasted_iota(I32, (C, C), 0).astype(F32)
    jj = lax.broadcasted_iota(I32, (C, C), 1).astype(F32)
    ri = lax.broadcasted_iota(I32, (C, RET_DK), 0).astype(F32)

    for h in range(RET_HEADS):
        qs = slice(h * RET_DK, (h + 1) * RET_DK)
        vs = slice(h * RET_DV, (h + 1) * RET_DV)
        for d, (q_ref, k_ref, v_ref, o_ref) in enumerate(((qf_ref, kf_ref, vf_ref, of_ref),
                                                         (qb_ref, kb_ref, vb_ref, ob_ref))):
            lg = lg_ref[d, h]
            q, k, v = q_ref[:, qs], k_ref[:, qs], v_ref[:, vs]
            if d == 0:
                dist = ii - jj
                q_dec = jnp.exp(lg * (ri + 1.0))
                k_dec = jnp.exp(lg * (C - 1.0 - ri))
            else:
                dist = jj - ii
                q_dec = jnp.exp(lg * (C - ri))
                k_dec = jnp.exp(lg * ri)
            decay = jnp.where(dist >= 0, jnp.exp(lg * jnp.maximum(dist, 0.0)), 0.0)
            att = (_dot_nt(q, k) * decay).astype(BF16)
            s = s_ref[d, h]
            o = _dot(att, v) + _dot((q.astype(F32) * q_dec).astype(BF16), s.astype(BF16))
            s_ref[d, h] = s * jnp.exp(lg * C) + _dot_tn((k.astype(F32) * k_dec).astype(BF16), v)
            o_ref[:, vs] = o.astype(o_ref.dtype)


def _retention(lg, rq, rk, z, dims):
    B, S, C, BS = dims["B"], dims["S"], dims["C"], dims["BS"]
    R = rq.shape[0]
    nc, nl = C // RET_CHUNK, S // RET_CHUNK
    base = BS // RET_CHUNK
    fwd = lambda b, c: jnp.where(c < nc, base + b * nc + c, b * nl + (c - nc))
    bwd = lambda b, c: jnp.where(c < nc, base + b * nc + (nc - 1 - c), b * nl + (nl - 1 - (c - nc)))
    vw = RET_HEADS * RET_DV
    spec = lambda w, col, f: pl.BlockSpec((RET_CHUNK, w), lambda b, c: (f(b, c), col))
    return pl.pallas_call(
        _ret_kernel,
        out_shape=[jax.ShapeDtypeStruct((R, vw), BF16), jax.ShapeDtypeStruct((R, vw), BF16)],
        grid=(B, nc + nl),
        in_specs=[pl.BlockSpec(memory_space=pltpu.SMEM),
                  spec(SEG, 0, fwd), spec(SEG, 0, fwd), spec(vw, C_RV * SEG // vw, fwd),
                  spec(SEG, 0, bwd), spec(SEG, 0, bwd), spec(vw, C_RV * SEG // vw, bwd)],
        out_specs=[spec(vw, 0, fwd), spec(vw, 0, bwd)],
        scratch_shapes=[pltpu.VMEM((2, RET_HEADS, RET_DK, RET_DV), F32)],
        compiler_params=_cparams(("parallel", "arbitrary")),
        name="retention",
    )(lg, rq, rk, z, rq, rk, z)


def _split_halves(q):
    lane = lax.broadcasted_iota(I32, q.shape, 1)
    lo = jnp.where(lane < 64, q, jnp.zeros_like(q))
    hi = jnp.where(lane >= 64, q, jnp.zeros_like(q))
    return jnp.concatenate([lo, hi], axis=0)


def _fold_lanes(x, op):
    acc = x[:, :128]
    for j in range(1, x.shape[1] // 128):
        acc = op(acc, x[:, j * 128:(j + 1) * 128])
    return acc


def _attend_unit(q_fn, sources, lanes, s_ref, e_ref, buf, out):
    chunks = []
    off = 0
    for k_ref, _, n_rows in sources:
        for c0 in range(0, n_rows, KEY_CHUNK):
            n = min(KEY_CHUNK, n_rows - c0)
            chunks.append((k_ref, c0, n, off))
            off += n
    q = q_fn()
    m = None
    for k_ref, r0, n, off in chunks:
        s = _dot_nt(q, k_ref[r0:r0 + n, lanes])
        s_ref[buf, :, off:off + n] = s
        blk = _fold_lanes(s, jnp.maximum)
        m = blk if m is None else jnp.maximum(m, blk)
        yield
    m = m.max(axis=-1, keepdims=True)
    l = jnp.zeros((ROW_BLOCK, 128), F32)
    for _, _, n, off in chunks:
        e = jnp.exp(s_ref[buf, :, off:off + n] - m)
        l = l + _fold_lanes(e, jnp.add)
        e_ref[buf, :, off:off + n] = e.astype(BF16)
        yield
    acc, off = None, 0
    for _, v_ref, n_rows in sources:
        t = _dot(e_ref[buf, :, off:off + n_rows], v_ref[:, lanes])
        acc = t if acc is None else acc + t
        off += n_rows
    out.append((acc, l.sum(axis=-1, keepdims=True)))
    yield


def _run_interleaved(units, n_a, n_b):
    for _ in range(n_a):
        next(units[0])
    for i, u in enumerate(units):
        nxt = units[i + 1] if i + 1 < len(units) else None
        a = 0
        for b in range(n_b):
            next(u, None)
            while nxt is not None and a < (b + 1) * n_a // n_b:
                next(nxt)
                a += 1


def _diff_kernel(lam_ref, q_ref, *refs, has_latent):
    if has_latent:
        kl_ref, vl_ref, kc_ref, vc_ref, o_ref, s_ref, e_ref = refs
        sources = ((kl_ref, vl_ref, kl_ref.shape[0]), (kc_ref, vc_ref, kc_ref.shape[0]))
    else:
        kc_ref, vc_ref, o_ref, s_ref, e_ref = refs
        sources = ((kc_ref, vc_ref, kc_ref.shape[0]),)
    lam = lam_ref[0, 0]
    tq = q_ref.shape[0]
    n_chunks = sum(-(-n_rows // KEY_CHUNK) for _, _, n_rows in sources)
    out, units = [], []
    for h in range(DIFF_HEADS):
        hs = slice(h * 128, (h + 1) * 128)
        for rb in range(0, 2 * tq, ROW_BLOCK):
            def q_fn(hs=hs, rb=rb):
                q = q_ref[rb % tq:rb % tq + ROW_BLOCK, hs]
                lane = lax.broadcasted_iota(I32, q.shape, 1)
                return jnp.where((lane < 64) if rb < tq else (lane >= 64), q, jnp.zeros_like(q))
            units.append(_attend_unit(q_fn, sources, hs, s_ref, e_ref, len(units) % 2, out))
    _run_interleaved(units, n_chunks, n_chunks + 1)
    per_head = 2 * tq // ROW_BLOCK
    for h in range(DIFF_HEADS):
        res = out[h * per_head:(h + 1) * per_head]
        acc = jnp.concatenate([r[0] for r in res], axis=0)
        l = jnp.concatenate([r[1] for r in res], axis=0)
        o = acc[:tq] * (1.0 / l[:tq]) - acc[tq:] * (lam / l[tq:])
        o_ref[:, h * 128:(h + 1) * 128] = o.astype(o_ref.dtype)


def _diff_attention(lam, dq, dk, z, dims, latent):
    B, S, C, BS = dims["B"], dims["S"], dims["C"], dims["BS"]
    lat_k = pl.BlockSpec((S, SEG), lambda b, i: (b, 0))
    lat_v = pl.BlockSpec((S, SEG), lambda b, i: (b, C_DV))
    ctx_k = pl.BlockSpec((C, SEG), lambda b, i: (BS // C + b, 0))
    ctx_v = pl.BlockSpec((C, SEG), lambda b, i: (BS // C + b, C_DV))
    if latent:
        tq = _tile(S, 256)
        nq = S // tq
        qmap = omap = lambda b, i: (b * nq + i, 0)
        in_specs = [lat_k, lat_v, ctx_k, ctx_v]
        args = (dk, z, dk, z)
        n_keys = S + C
    else:
        tq, nq = C, 1
        qmap = lambda b, i: (BS // C + b, 0)
        omap = lambda b, i: (b, 0)
        in_specs = [ctx_k, ctx_v]
        args = (dk, z)
        n_keys = C
    assert tq % ROW_BLOCK == 0
    return pl.pallas_call(
        functools.partial(_diff_kernel, has_latent=latent),
        out_shape=jax.ShapeDtypeStruct((B * nq * tq, SEG), BF16),
        grid=(B, nq),
        in_specs=[pl.BlockSpec(memory_space=pltpu.SMEM), pl.BlockSpec((tq, SEG), qmap)] + in_specs,
        out_specs=pl.BlockSpec((tq, SEG), omap),
        scratch_shapes=[pltpu.VMEM((2, ROW_BLOCK, n_keys), F32), pltpu.VMEM((2, ROW_BLOCK, n_keys), BF16)],
        compiler_params=_cparams(("parallel", "arbitrary")),
        name="diff_lat" if latent else "diff_ctx",
    )(lam, dq, *args)


def _na_kernel(q_ref, *refs, has_local, rows):
    if has_local:
        kl_ref, vl_ref, kc_ref, vc_ref, tidx_ref, tile_ref, o_ref = refs
        qb = pl.program_id(1)
        kr0 = jnp.clip(qb * NA_QROWS - (NA_KROWS - NA_QROWS) // 2, 0, rows - NA_KROWS)
        k0 = pl.multiple_of(kr0 * GRID_W, 256)
        nk = NA_KROWS * GRID_W
        last = rows // NA_QROWS - 1
        btype = jnp.where(qb == 0, 0, jnp.where(qb == last, 2, 1))

        def head_bias(h):
            row_blocks = []
            for i in range(NA_QROWS):
                base = (btype * NA_QROWS + i) * (NA_KROWS // 2)
                row_blocks.append(jnp.concatenate(
                    [tile_ref[0, h, tidx_ref[base + jp]] for jp in range(NA_KROWS // 2)], axis=1))
            return jnp.concatenate(row_blocks, axis=0)
    else:
        kc_ref, vc_ref, o_ref = refs
    tq = q_ref.shape[0]
    lane = lax.broadcasted_iota(I32, (tq, 128), 1)
    for p in range(NA_HEADS // 2):
        ps = slice(p * 128, (p + 1) * 128)
        q2 = _split_halves(q_ref[:, ps])
        parts = [_dot_nt(q2, kc_ref[:, ps])]
        vals = [vc_ref[:, ps]]
        if has_local:
            s_loc = _dot_nt(q2, kl_ref[pl.ds(k0, nk), ps])
            bias = jnp.concatenate([head_bias(2 * p), head_bias(2 * p + 1)], axis=0).astype(F32)
            parts.append(s_loc + bias)
            vals.append(vl_ref[pl.ds(k0, nk), ps])
        m = parts[0].max(axis=-1, keepdims=True)
        for s in parts[1:]:
            m = jnp.maximum(m, s.max(axis=-1, keepdims=True))
        es = [jnp.exp(s - m) for s in parts]
        den = es[0].sum(axis=-1, keepdims=True)
        for e in es[1:]:
            den = den + e.sum(axis=-1, keepdims=True)
        o = None
        for e, v in zip(es, vals):
            t = _dot(e.astype(BF16), v)
            o = t if o is None else o + t
        o = o * (1.0 / den)
        o_ref[:, ps] = jnp.where(lane < 64, o[:tq], o[tq:]).astype(o_ref.dtype)


def _na_attention(nq_, nk_, z, bias, layer, dims, latent):
    B, S, C, BS = dims["B"], dims["S"], dims["C"], dims["BS"]
    ctx_k = pl.BlockSpec((C, SEG), lambda b, i: (BS // C + b, 0))
    ctx_v = pl.BlockSpec((C, SEG), lambda b, i: (BS // C + b, C_NV))
    if latent:
        tq = NA_QROWS * GRID_W
        nq = S // tq
        qmap = omap = lambda b, i: (b * nq + i, 0)
        tiles, tidx = bias
        in_specs = [pl.BlockSpec((S, SEG), lambda b, i: (b, 0)),
                    pl.BlockSpec((S, SEG), lambda b, i: (b, C_NV)),
                    ctx_k, ctx_v,
                    pl.BlockSpec(memory_space=pltpu.SMEM),
                    pl.BlockSpec((1,) + tiles.shape[1:], lambda b, i: (layer, 0, 0, 0, 0))]
        args = (nk_, z, nk_, z, tidx, tiles)
    else:
        tq, nq = C, 1
        qmap = lambda b, i: (BS // C + b, 0)
        omap = lambda b, i: (b, 0)
        in_specs = [ctx_k, ctx_v]
        args = (nk_, z)
    return pl.pallas_call(
        functools.partial(_na_kernel, has_local=latent, rows=S // GRID_W),
        out_shape=jax.ShapeDtypeStruct((B * nq * tq, SEG), BF16),
        grid=(B, nq),
        in_specs=[pl.BlockSpec((tq, SEG), qmap)] + in_specs,
        out_specs=pl.BlockSpec((tq, SEG), omap),
        compiler_params=_cparams(("parallel", "arbitrary")),
        name="na_lat" if latent else "na_ctx",
    )(nq_, *args)


def _na_bias_tables(rpb, S):
    rows = S // GRID_W
    nqb = rows // NA_QROWS
    n_off = 2 * NA_KR - 1
    i = np.arange(NA_QROWS)[:, None]
    j = np.arange(NA_KROWS)[None, :]
    code = np.zeros((3, NA_QROWS, NA_KROWS), np.int64)
    for t, qb in enumerate((0, min(1, nqb - 1), nqb - 1)):
        r = qb * NA_QROWS + i
        kr0 = np.clip(qb * NA_QROWS - (NA_KROWS - NA_QROWS) // 2, 0, rows - NA_KROWS)
        kr = kr0 + j
        r0 = np.clip(r - NA_KR // 2, 0, rows - NA_KR)
        ok = (kr >= r0) & (kr < r0 + NA_KR)
        code[t] = np.where(ok, kr - r + NA_KR - 1, n_off)
    pairs = code.reshape(3, NA_QROWS, NA_KROWS // 2, 2)
    uniq, inverse = np.unique(pairs.reshape(-1, 2), axis=0, return_inverse=True)
    cq = np.arange(GRID_W)[:, None]
    ck = np.arange(GRID_W)[None, :]
    c0 = np.clip(cq - NA_KW // 2, 0, GRID_W - NA_KW)
    okc = (ck >= c0) & (ck < c0 + NA_KW)
    offc = np.clip(ck - cq + NA_KW - 1, 0, 2 * NA_KW - 2)
    sel_c = np.zeros((GRID_W, GRID_W, 2 * NA_KW - 1), np.float32)
    sel_c[np.broadcast_to(cq, okc.shape)[okc], np.broadcast_to(ck, okc.shape)[okc], offc[okc]] = 1.0
    planes = jnp.einsum("lhab,qkb->lhaqk", rpb.astype(F32), jnp.asarray(sel_c), precision=lax.Precision.HIGHEST)
    planes = jnp.where(jnp.asarray(okc)[None, None, None], planes, NEG)
    planes = jnp.concatenate([planes, jnp.full_like(planes[:, :, :1], NEG)], axis=2)
    tiles = jnp.concatenate([planes[:, :, uniq[:, 0]], planes[:, :, uniq[:, 1]]], axis=-1)
    return tiles.astype(BF16), jnp.asarray(inverse.reshape(-1), I32)


def _head_norm(x, width, gain):
    outs = []
    for h in range(x.shape[-1] // width):
        xh = x[:, h * width:(h + 1) * width]
        outs.append(xh * lax.rsqrt(jnp.mean(xh * xh, axis=-1, keepdims=True) + EPS))
    return jnp.concatenate(outs, axis=-1) * gain


def _merge_kernel(x_ref, gate_ref, rf_ref, rb_ref, rg_ref, *refs, diff_scale, n_lat_tiles):
    if n_lat_tiles is None:
        od_ref, on_ref = refs[:2]
        od, on = od_ref[...], on_ref[...]
    else:
        odl_ref, onl_ref, odc_ref, onc_ref = refs[:4]
        is_lat = pl.program_id(0) < n_lat_tiles
        od = jnp.where(is_lat, odl_ref[...], odc_ref[...])
        on = jnp.where(is_lat, onl_ref[...], onc_ref[...])
    g1_ref, g2_ref, g3_ref, retg_ref, dg_ref, wr_ref, wd_ref, wn_ref, wo_ref, o_ref = refs[-10:]
    r = _head_norm(rf_ref[...].astype(F32) + rb_ref[...].astype(F32), RET_DV, retg_ref[...])
    rg = rg_ref[...].astype(F32)
    y_ret = _dot((rg * _sigmoid(rg) * r).astype(BF16), wr_ref[0])
    d = _head_norm(od.astype(F32), DIFF_DV, dg_ref[...]) * diff_scale
    y_diff = _dot(d.astype(BF16), wd_ref[0])
    y_na = _dot(on, wn_ref[0])
    m = (_sigmoid(g1_ref[...].astype(F32)) * y_ret + _sigmoid(g2_ref[...].astype(F32)) * y_diff
         + _sigmoid(g3_ref[...].astype(F32)) * y_na)
    y = _dot(m.astype(BF16), wo_ref[0])
    o_ref[...] = x_ref[...] + gate_ref[0] * y


def _merge(x, gate, rf, rb, z, attn, ret_g, d_g, wr, wd, wn, wo, layer, diff_scale, dims, rows):
    D = x.shape[1]
    tm = _tile(dims["BC"], 256)
    nlt, tpb, B = dims["BS"] // tm, dims["S"] // tm, dims["B"]
    rw = RET_HEADS * RET_DV
    gcol = C_GATES * SEG // D
    row = lambda w, col=0: pl.BlockSpec((tm, w), lambda i: (i, col))
    lat = pl.BlockSpec((tm, SEG), lambda i: (jnp.minimum(i, nlt - 1), 0))
    ctx = pl.BlockSpec((tm, SEG), lambda i: (jnp.maximum(i - nlt, 0), 0))
    const = lambda a: pl.BlockSpec(a.shape, lambda i: (0,) * a.ndim)
    wspec = lambda a: pl.BlockSpec((1,) + a.shape[1:], lambda i: (layer, 0, 0))
    with_ctx = len(attn) == 4
    return pl.pallas_call(
        functools.partial(_merge_kernel, diff_scale=diff_scale, n_lat_tiles=nlt if with_ctx else None),
        out_shape=jax.ShapeDtypeStruct((rows, D), F32),
        grid=(rows // tm,),
        in_specs=[row(D), pl.BlockSpec((1, 1, D), lambda i: (_mod_index(i, nlt, tpb, B), 0, 0)),
                  row(rw), row(rw), row(rw, C_RG * SEG // rw)]
                 + ([lat, lat, ctx, ctx] if with_ctx else [row(SEG), row(SEG)])
                 + [row(D, gcol), row(D, gcol + 1), row(D, gcol + 2),
                    const(ret_g), const(d_g), wspec(wr), wspec(wd), wspec(wn), wspec(wo)],
        out_specs=row(D),
        compiler_params=_cparams(("parallel",)),
        name="merge",
    )(x, gate, rf, rb, z, *attn, z, z, z, ret_g, d_g, wr, wd, wn, wo)


def _to_token_tiles(ref, x):
    rows, width = x.shape
    n = width // 128
    for c in range(n):
        ref[pl.ds(c, rows, stride=n), :] = x[:, c * 128:(c + 1) * 128]


def _from_token_tiles(ref, rows):
    n = ref.shape[0] // rows
    return jnp.concatenate([ref[pl.ds(c, rows, stride=n), :] for c in range(n)], axis=1)


def _router_kernel(x_ref, g_ref, sh_ref, sc_ref, wrt_ref, br_ref, h_ref, idx_ref, wts_ref, cnt_ref, run_ref):
    i = pl.program_id(0)
    tm = x_ref.shape[0]
    E = N_EXPERTS

    @pl.when(i == 0)
    def _():
        run_ref[...] = jnp.zeros_like(run_ref)

    h = _modnorm(x_ref[...], g_ref[...], sh_ref[0], sc_ref[0])
    _to_token_tiles(h_ref, h)
    h_hi = h.astype(BF16)
    h_lo = (h - h_hi.astype(F32)).astype(BF16)
    w = wrt_ref[...]
    w_hi = w.astype(BF16)
    w_lo = (w - w_hi.astype(F32)).astype(BF16)
    logits = _dot_nt(w_hi, h_hi) + _dot_nt(w_hi, h_lo) + _dot_nt(w_lo, h_hi)
    s = _sigmoid(logits)
    sel = s + br_ref[...]

    best, gi = None, None
    for g in range(N_GROUPS):
        r = [sel[g * EXPERTS_PER_GROUP + k:g * EXPERTS_PER_GROUP + k + 1] for k in range(EXPERTS_PER_GROUP)]
        top2 = None
        for a in range(EXPERTS_PER_GROUP):
            for b in range(a + 1, EXPERTS_PER_GROUP):
                pair = r[a] + r[b]
                top2 = pair if top2 is None else jnp.maximum(top2, pair)
        if best is None:
            best, gi = top2, jnp.zeros(top2.shape, I32)
        else:
            upd = top2 > best
            gi = jnp.where(upd, g, gi)
            best = jnp.where(upd, top2, best)

    eid = lax.broadcasted_iota(I32, (E, tm), 0)
    masked = jnp.where(eid // EXPERTS_PER_GROUP == gi, sel, -jnp.inf)
    m1 = masked.max(axis=0, keepdims=True)
    i1 = jnp.where(masked == m1, eid, E).min(axis=0, keepdims=True)
    masked2 = jnp.where(eid == i1, -jnp.inf, masked)
    m2 = masked2.max(axis=0, keepdims=True)
    i2 = jnp.where(masked2 == m2, eid, E).min(axis=0, keepdims=True)
    oh1 = eid == i1
    oh2 = eid == i2
    w1 = jnp.where(oh1, s, 0.0).sum(axis=0, keepdims=True)
    w2 = jnp.where(oh2, s, 0.0).sum(axis=0, keepdims=True)
    tot = w1 + w2

    oh = (oh1 | oh2).astype(F32)
    before = (lax.broadcasted_iota(I32, (tm, tm), 0) < lax.broadcasted_iota(I32, (tm, tm), 1)).astype(BF16)
    prefix = _dot(oh.astype(BF16), before) + run_ref[:, 0:1]
    rank1 = jnp.where(oh1, prefix, 0.0).sum(axis=0, keepdims=True)
    rank2 = jnp.where(oh2, prefix, 0.0).sum(axis=0, keepdims=True)
    run_ref[...] = run_ref[...] + oh.sum(axis=1, keepdims=True)

    idx_ref[...] = jnp.zeros_like(idx_ref)
    wts_ref[...] = jnp.zeros_like(wts_ref)
    for k, v in enumerate((i1, i2, rank1.astype(I32), rank2.astype(I32))):
        idx_ref[0, k:k + 1, :] = v
    wts_ref[0, 0:1, :] = w1 / tot
    wts_ref[0, 1:2, :] = w2 / tot
    cnt_ref[...] = run_ref[...]


def _router(x, g, shift, scale, wrt, br, dims, rows):
    D = x.shape[1]
    tm = _tile(dims["BC"], 256)
    nt = rows // tm
    nlt, tpb, B = dims["BS"] // tm, dims["S"] // tm, dims["B"]
    mod = lambda i: (_mod_index(i, nlt, tpb, B), 0, 0)
    n = D // 128
    return pl.pallas_call(
        _router_kernel,
        out_shape=[jax.ShapeDtypeStruct((rows * n, 128), F32),
                   jax.ShapeDtypeStruct((nt, 8, tm), I32),
                   jax.ShapeDtypeStruct((nt, 8, tm), F32),
                   jax.ShapeDtypeStruct((N_EXPERTS, 128), F32)],
        grid=(nt,),
        in_specs=[pl.BlockSpec((tm, D), lambda i: (i, 0)),
                  pl.BlockSpec((1, D), lambda i: (0, 0)),
                  pl.BlockSpec((1, 1, D), mod), pl.BlockSpec((1, 1, D), mod),
                  pl.BlockSpec((N_EXPERTS, D), lambda i: (0, 0)),
                  pl.BlockSpec((N_EXPERTS, 1), lambda i: (0, 0))],
        out_specs=[pl.BlockSpec((tm * n, 128), lambda i: (i, 0)),
                   pl.BlockSpec((1, 8, tm), lambda i: (i, 0, 0)),
                   pl.BlockSpec((1, 8, tm), lambda i: (i, 0, 0)),
                   pl.BlockSpec((N_EXPERTS, 128), lambda i: (0, 0))],
        scratch_shapes=[pltpu.VMEM((N_EXPERTS, 128), F32)],
        compiler_params=_cparams(("arbitrary",)),
        name="router",
    )(x, g, shift, scale, wrt, br)


GATHER_UNROLL = 16


def _token_copy(r, n, idx_ref, src_hbm, dst_ref, sem):
    src = pl.multiple_of(idx_ref[0, 0, r] * n, n)
    dst = pl.multiple_of(r * n, n)
    return pltpu.make_async_copy(src_hbm.at[pl.ds(src, n)], dst_ref.at[pl.ds(dst, n)], sem)


def _start_gather(n_tok, n, idx_ref, src_hbm, dst_ref, sem):
    def body(g, carry):
        for u in range(GATHER_UNROLL):
            _token_copy(g * GATHER_UNROLL + u, n, idx_ref, src_hbm, dst_ref, sem).start(priority=u % 2)
        return carry

    lax.fori_loop(0, n_tok // GATHER_UNROLL, body, 0)


def _wait_gather(n_tok, n, idx_ref, src_hbm, dst_ref, sem):
    def body(g, carry):
        for u in range(GATHER_UNROLL):
            _token_copy(g * GATHER_UNROLL + u, n, idx_ref, src_hbm, dst_ref, sem).wait()
        return carry

    lax.fori_loop(0, n_tok // GATHER_UNROLL, body, 0)


def _pipelined_gathers(streams, n_tok, n):
    t = pl.program_id(0)
    last = pl.num_programs(0) - 1
    slot = lax.rem(t, 2)

    @pl.when(t == 0)
    def _():
        for cur, _, src, buf, sem in streams:
            _start_gather(n_tok, n, cur, src, buf.at[0], sem.at[0])

    @pl.when(t < last)
    def _():
        for _, nxt, src, buf, sem in streams:
            _start_gather(n_tok, n, nxt, src, buf.at[1 - slot], sem.at[1 - slot])

    for cur, _, src, buf, sem in streams:
        _wait_gather(n_tok, n, cur, src, buf.at[slot], sem.at[slot])
    return slot


def _expert_kernel(te_ref, cur_ref, nxt_ref, h_hbm, wg_ref, wu_ref, wd_ref, o_ref, x_buf, sem, *, rows, n):
    slot = _pipelined_gathers(((cur_ref, nxt_ref, h_hbm, x_buf, sem),), rows, n)
    x = _from_token_tiles(x_buf.at[slot], rows).astype(BF16)
    gate = _dot(x, wg_ref[0, 0])
    a = gate * _sigmoid(gate) * _dot(x, wu_ref[0, 0])
    _to_token_tiles(o_ref, _dot(a.astype(BF16), wd_ref[0, 0]))


def _experts(tile_expert, src_rows, h, wg, wu, wd, layer, tmE):
    D, De = wg.shape[2:]
    n = D // 128
    P = src_rows.shape[0]
    nt = P // tmE
    wspec = lambda a, b: pl.BlockSpec((1, 1, a, b), lambda t, te: (layer, te[t], 0, 0))
    idx = src_rows.reshape(nt, 1, tmE)
    cur = pl.BlockSpec((1, 1, tmE), lambda t, te: (t, 0, 0), memory_space=pltpu.SMEM)
    nxt = pl.BlockSpec((1, 1, tmE), lambda t, te: (jnp.minimum(t + 1, nt - 1), 0, 0), memory_space=pltpu.SMEM)
    return pl.pallas_call(
        functools.partial(_expert_kernel, rows=tmE, n=n),
        out_shape=jax.ShapeDtypeStruct((P * n, 128), F32),
        grid_spec=pltpu.PrefetchScalarGridSpec(
            num_scalar_prefetch=1,
            grid=(nt,),
            in_specs=[cur, nxt, pl.BlockSpec(memory_space=pl.ANY), wspec(D, De), wspec(D, De), wspec(De, D)],
            out_specs=pl.BlockSpec((tmE * n, 128), lambda t, te: (t, 0)),
            scratch_shapes=[pltpu.VMEM((2, tmE * n, 128), F32), pltpu.SemaphoreType.DMA((2,))]),
        compiler_params=_cparams(("arbitrary",)),
        name="experts",
    )(tile_expert, idx, idx, h, wg, wu, wd)


def _combine_kernel(c1_ref, n1_ref, c2_ref, n2_ref, x_ref, gate_ref, w_ref, ye_hbm, o_ref,
                    b1_ref, b2_ref, sem1, sem2):
    rows, width = x_ref.shape
    n = width // 128
    slot = _pipelined_gathers(((c1_ref, n1_ref, ye_hbm, b1_ref, sem1), (c2_ref, n2_ref, ye_hbm, b2_ref, sem2)),
                              rows, n)
    w = w_ref[...]
    y = (w[:, 0:1] * _from_token_tiles(b1_ref.at[slot], rows)
         + w[:, 1:2] * _from_token_tiles(b2_ref.at[slot], rows))
    o_ref[...] = x_ref[...] + gate_ref[0] * y


def _combine(pos1, pos2, x, gate, w, ye, dims, rows):
    D = x.shape[1]
    n = D // 128
    tc = _tile(dims["BC"], 256)
    nt = rows // tc
    nlt, tpb, B = dims["BS"] // tc, dims["S"] // tc, dims["B"]
    cur = pl.BlockSpec((1, 1, tc), lambda i: (i, 0, 0), memory_space=pltpu.SMEM)
    nxt = pl.BlockSpec((1, 1, tc), lambda i: (jnp.minimum(i + 1, nt - 1), 0, 0), memory_space=pltpu.SMEM)
    p1, p2 = pos1.reshape(nt, 1, tc), pos2.reshape(nt, 1, tc)
    return pl.pallas_call(
        _combine_kernel,
        out_shape=jax.ShapeDtypeStruct((rows, D), F32),
        grid=(nt,),
        in_specs=[cur, nxt, cur, nxt,
                  pl.BlockSpec((tc, D), lambda i: (i, 0)),
                  pl.BlockSpec((1, 1, D), lambda i: (_mod_index(i, nlt, tpb, B), 0, 0)),
                  pl.BlockSpec((tc, 2), lambda i: (i, 0)),
                  pl.BlockSpec(memory_space=pl.ANY)],
        out_specs=pl.BlockSpec((tc, D), lambda i: (i, 0)),
        scratch_shapes=[pltpu.VMEM((2, tc * n, 128), F32), pltpu.VMEM((2, tc * n, 128), F32),
                        pltpu.SemaphoreType.DMA((2,)), pltpu.SemaphoreType.DMA((2,))],
        compiler_params=_cparams(("arbitrary",)),
        name="combine",
    )(p1, p1, p2, p2, x, gate, w, ye)


def _moe(x, g, shift, scale, gate, wrt, br, wg, wu, wd, layer, dims, rows):
    tmE = 256
    h, idx, wts, cnt = _router(x, g, shift, scale, wrt, br, dims, rows)
    nt = idx.shape[0]
    flat = lambda a, k: a[:, k, :].reshape(rows)
    e1, e2, r1, r2 = (flat(idx, k) for k in range(4))
    counts = cnt[:, 0].astype(I32)
    padded = (counts + tmE - 1) // tmE * tmE
    ends = jnp.cumsum(padded)
    starts = ends - padded
    pos1 = starts[e1] + r1
    pos2 = starts[e2] + r2
    P = (2 * rows + N_EXPERTS * tmE) // tmE * tmE
    tok = jnp.arange(rows, dtype=I32)
    src = jnp.zeros((P,), I32).at[jnp.concatenate([pos1, pos2])].set(
        jnp.concatenate([tok, tok]), unique_indices=True, mode="promise_in_bounds")
    tile_start = jnp.arange(P // tmE, dtype=I32) * tmE
    tile_expert = jnp.minimum(jnp.sum(ends[None, :] <= tile_start[:, None], axis=1), N_EXPERTS - 1).astype(I32)
    ye = _experts(tile_expert, src, h, wg, wu, wd, layer, tmE)
    w = jnp.stack([flat(wts, 0), flat(wts, 1)], axis=-1)
    return _combine(pos1, pos2, x, gate, w, ye, dims, rows)


def kernel(x, c, ctx, c_ctx, w_ada, b_ada, norm1_g, norm2_g, w_in, ret_decay_logit, ret_norm_g, diff_q_norm_g, diff_k_norm_g, diff_lambda, diff_norm_g, na_q_norm_g, na_k_norm_g, na_rpb, w_ret_out, w_diff_out, w_na_out, w_out, w_router, b_router, w_exp_gate, w_exp_up, w_exp_down):
    B, S, D = x.shape
    C = ctx.shape[1]
    L = w_in.shape[0]
    BS, BC = B * S, B * C
    assert S % (NA_QROWS * GRID_W) == 0 and S // GRID_W >= NA_KROWS and C % RET_CHUNK == 0
    assert (C_GATES * SEG) % D == 0 and w_in.shape[2] == C_GATES * SEG + 3 * D
    dims = dict(B=B, S=S, C=C, BS=BS, BC=BC)

    n_c = -(-(B + 1) // 8) * 8
    c_all = jnp.concatenate([c, c_ctx[None], jnp.zeros((n_c - B - 1, D), F32)], axis=0)
    mods = _ada(c_all, w_ada, b_ada)[:, :B + 1]
    mod = lambda l, k: mods[l, :, k * D:(k + 1) * D].reshape(B + 1, 1, D)

    xa = jnp.concatenate([x.reshape(BS, D), ctx.reshape(BC, D)], axis=0)

    tm_prep = _tile(BC, 512)
    tables = _rope_tables(S, tm_prep)
    gmat = jnp.asarray(np.kron(np.eye(SEG // 64), np.full((64, 64), 1.0 / 64)), BF16)
    bias = _na_bias_tables(na_rpb, S)
    lg = jax.nn.log_sigmoid(ret_decay_logit.astype(F32))
    lam_p = diff_lambda.astype(F32)
    wrt = w_router.T
    br = b_router.astype(F32).reshape(N_EXPERTS, 1)
    tile8 = lambda v: jnp.tile(v.astype(F32), SEG // v.shape[0])
    row = lambda v: v.astype(F32).reshape(1, -1)
    w_in_b, w_ret_b, w_diff_b, w_na_b, w_out_b, w_gate_b, w_up_b, w_down_b = (
        w.astype(BF16) for w in (w_in, w_ret_out, w_diff_out, w_na_out, w_out, w_exp_gate, w_exp_up, w_exp_down))

    for l in range(L):
        last = l == L - 1
        lam_init = 0.8 - 0.6 * math.exp(-0.3 * l)
        lam = (jnp.exp(jnp.sum(lam_p[l, 0] * lam_p[l, 1])) - jnp.exp(jnp.sum(lam_p[l, 2] * lam_p[l, 3]))
               + lam_init).reshape(1, 1)
        z = _inproj(xa, row(norm1_g[l]), mod(l, 0), mod(l, 1), w_in_b, l, dims)
        gains = jnp.stack([tile8(diff_q_norm_g[l]), tile8(diff_k_norm_g[l]),
                           tile8(na_q_norm_g[l]), tile8(na_k_norm_g[l])] + [jnp.zeros((SEG,), F32)] * 4)
        rq, rk, dq, dk, nq, nk = _prep(z, gains, gmat, tables, dims)

        rf, rb = _retention(lg[l], rq, rk, z, dims)
        attn = (_diff_attention(lam, dq, dk, z, dims, True), _na_attention(nq, nk, z, bias, l, dims, True))
        rows = BS if last else BS + BC
        if not last:
            attn += (_diff_attention(lam, dq, dk, z, dims, False), _na_attention(nq, nk, z, None, l, dims, False))
        x1 = _merge(xa, mod(l, 2), rf, rb, z, attn, row(ret_norm_g[l]), row(diff_norm_g[l]),
                    w_ret_b, w_diff_b, w_na_b, w_out_b, l, 1.0 - lam_init, dims, rows)
        xa = _moe(x1, row(norm2_g[l]), mod(l, 3), mod(l, 4), mod(l, 5), wrt, br,
                  w_gate_b, w_up_b, w_down_b, l, dims, rows)
    return xa.reshape(B, S, D)
```

```python
import functools
import math

import numpy as np
import jax
import jax.numpy as jnp
from jax import lax
from jax.experimental import pallas as pl
from jax.experimental.pallas import tpu as pltpu

F32 = jnp.float32
BF16 = jnp.bfloat16
I32 = jnp.int32

GRID_W = 64
RET_HEADS, RET_DK, RET_DV, RET_CHUNK = 4, 128, 256, 128
DIFF_HEADS, DIFF_HD, DIFF_DV = 4, 64, 128
NA_HEADS, NA_HD, NA_KR, NA_KW = 8, 64, 8, 16
N_EXPERTS, N_GROUPS = 16, 4
EXPERTS_PER_GROUP = N_EXPERTS // N_GROUPS
ROPE_BASE = 10000.0
EPS = 1e-6
NEG = -1e30

SEG = 512
C_RQ, C_RK, C_RV, C_RG, C_DQ, C_DK, C_DV, C_NQ, C_NK, C_NV, C_GATES = 0, 1, 2, 4, 6, 7, 8, 9, 10, 11, 12
NA_QROWS = 8
NA_KROWS = 16
KEY_CHUNK = 256
ROW_BLOCK = 256

V7X_VMEM_BYTES = 64 * 1024 * 1024
VMEM_LIMIT = V7X_VMEM_BYTES - 8 * 1024 * 1024


def _cparams(sem):
    return pltpu.CompilerParams(dimension_semantics=sem, vmem_limit_bytes=VMEM_LIMIT)


def _sigmoid(x):
    return 1.0 / (1.0 + jnp.exp(-x))


def _tile(n, pref):
    t = min(n, pref)
    while n % t:
        t //= 2
    return t


def _dot_nt(a, b):
    return lax.dot_general(a, b, (((1,), (1,)), ((), ())), preferred_element_type=F32)


def _dot_tn(a, b):
    return lax.dot_general(a, b, (((0,), (0,)), ((), ())), preferred_element_type=F32)


def _dot(a, b):
    return jnp.dot(a, b, preferred_element_type=F32)


def _ada_kernel(c_ref, w_ref, b_ref, o_ref):
    c = c_ref[...]
    a = (c * _sigmoid(c)).astype(BF16)
    o_ref[0] = _dot(a, w_ref[0].astype(BF16)) + b_ref[0]


def _ada(c_all, w_ada, b_ada):
    L, D, N = w_ada.shape
    M = c_all.shape[0]
    tn = _tile(N, 1024)
    return pl.pallas_call(
        _ada_kernel,
        out_shape=jax.ShapeDtypeStruct((L, M, N), F32),
        grid=(L, N // tn),
        in_specs=[
            pl.BlockSpec((M, D), lambda l, j: (0, 0)),
            pl.BlockSpec((1, D, tn), lambda l, j: (l, 0, j)),
            pl.BlockSpec((1, 1, tn), lambda l, j: (l, 0, j)),
        ],
        out_specs=pl.BlockSpec((1, M, tn), lambda l, j: (l, 0, j)),
        compiler_params=_cparams(("parallel", "parallel")),
        name="ada",
    )(c_all, w_ada, b_ada.reshape(L, 1, N))


def _modnorm(x, g, shift, scale):
    ms = jnp.mean(x * x, axis=-1, keepdims=True)
    y = x * lax.rsqrt(ms + EPS) * g
    return y * (1.0 + scale) + shift


def _inproj_kernel(x_ref, g_ref, sh_ref, sc_ref, w_ref, o_ref, h_ref):
    @pl.when(pl.program_id(1) == 0)
    def _():
        h_ref[...] = _modnorm(x_ref[...], g_ref[...], sh_ref[0], sc_ref[0]).astype(BF16)

    o_ref[...] = _dot(h_ref[...], w_ref[0]).astype(o_ref.dtype)


def _mod_index(i, n_lat_tiles, tiles_per_batch, n_batch):
    return jnp.where(i < n_lat_tiles, i // tiles_per_batch, n_batch)


def _inproj(x, g, shift, scale, w, layer, dims):
    R, D = x.shape
    P = w.shape[2]
    tm = _tile(dims["BC"], 1024)
    tn = _tile(P, 1024)
    nlt, tpb, B = dims["BS"] // tm, dims["S"] // tm, dims["B"]
    mod = lambda i, j: (_mod_index(i, nlt, tpb, B), 0, 0)
    return pl.pallas_call(
        _inproj_kernel,
        out_shape=jax.ShapeDtypeStruct((R, P), BF16),
        grid=(R // tm, P // tn),
        in_specs=[
            pl.BlockSpec((tm, D), lambda i, j: (i, 0)),
            pl.BlockSpec((1, D), lambda i, j: (0, 0)),
            pl.BlockSpec((1, 1, D), mod),
            pl.BlockSpec((1, 1, D), mod),
            pl.BlockSpec((1, D, tn), lambda i, j: (layer, 0, j)),
        ],
        out_specs=pl.BlockSpec((tm, tn), lambda i, j: (i, j)),
        scratch_shapes=[pltpu.VMEM((tm, D), BF16)],
        compiler_params=_cparams(("parallel", "arbitrary")),
        name="inproj",
    )(x, g, shift, scale, w)


def _group_mean_sq(x, gmat):
    xx = x * x
    hi = xx.astype(BF16)
    lo = (xx - hi.astype(F32)).astype(BF16)
    return _dot(hi, gmat) + _dot(lo, gmat)


def _rope(x, cos, sin, half):
    w = x.shape[-1]
    lane = lax.broadcasted_iota(I32, x.shape, 1)
    first = (lane % (2 * half)) < half
    swapped = jnp.where(first, pltpu.roll(x, w - half, 1), pltpu.roll(x, half, 1))
    return x * cos + swapped * sin


def _prep_kernel(rq_ref, rk_ref, dq_ref, dk_ref, nq_ref, nk_ref, gains_ref, gmat_ref, tab_ref,
                 orq_ref, ork_ref, odq_ref, odk_ref, onq_ref, onk_ref):
    reps = SEG // 128
    cos_r = jnp.concatenate([tab_ref[0]] * reps, axis=1)
    sin_r = jnp.concatenate([tab_ref[1]] * reps, axis=1)
    cos_d = jnp.concatenate([tab_ref[2]] * reps, axis=1)
    sin_d = jnp.concatenate([tab_ref[3]] * reps, axis=1)
    gmat = gmat_ref[...]

    def normed(ref, row):
        x = ref[...].astype(F32)
        return x * lax.rsqrt(_group_mean_sq(x, gmat) + EPS) * gains_ref[row:row + 1, :]

    orq_ref[...] = _rope(rq_ref[...].astype(F32), cos_r, sin_r, RET_DK // 4).astype(BF16)
    ork_ref[...] = _rope(rk_ref[...].astype(F32) * (RET_DK ** -0.5), cos_r, sin_r, RET_DK // 4).astype(BF16)
    odq_ref[...] = (_rope(normed(dq_ref, 0), cos_d, sin_d, DIFF_HD // 4) * (DIFF_HD ** -0.5)).astype(BF16)
    odk_ref[...] = _rope(normed(dk_ref, 1), cos_d, sin_d, DIFF_HD // 4).astype(BF16)
    onq_ref[...] = (normed(nq_ref, 2) * (NA_HD ** -0.5)).astype(BF16)
    onk_ref[...] = normed(nk_ref, 3).astype(BF16)


def _prep(z, gains, gmat, tables, dims):
    R = z.shape[0]
    tm = _tile(dims["BC"], 512)
    nlt, tpb = dims["BS"] // tm, dims["S"] // tm
    zspec = lambda c: pl.BlockSpec((tm, SEG), lambda i: (i, c))
    tab = lambda i: (0, jnp.where(i < nlt, i % tpb, tpb), 0)
    ospec = pl.BlockSpec((tm, SEG), lambda i: (i, 0))
    return pl.pallas_call(
        _prep_kernel,
        out_shape=[jax.ShapeDtypeStruct((R, SEG), BF16)] * 6,
        grid=(R // tm,),
        in_specs=[zspec(C_RQ), zspec(C_RK), zspec(C_DQ), zspec(C_DK), zspec(C_NQ), zspec(C_NK),
                  pl.BlockSpec((8, SEG), lambda i: (0, 0)),
                  pl.BlockSpec((SEG, SEG), lambda i: (0, 0)),
                  pl.BlockSpec((4, tm, 128), tab)],
        out_specs=[ospec] * 6,
        compiler_params=_cparams(("parallel",)),
        name="prep",
    )(z, z, z, z, z, z, gains, gmat, tables)


def _rope_tables(S, tm):
    t = jnp.arange(S)
    row, col = t // GRID_W, t % GRID_W

    def one(d, width):
        inv = jnp.power(ROPE_BASE, -jnp.arange(0, d, 2, dtype=F32) / d)
        ar = row.astype(F32)[:, None] * inv[None, :]
        ac = col.astype(F32)[:, None] * inv[None, :]
        cos = jnp.concatenate([jnp.cos(ar), jnp.cos(ar), jnp.cos(ac), jnp.cos(ac)], axis=-1)
        sin = jnp.concatenate([-jnp.sin(ar), jnp.sin(ar), -jnp.sin(ac), jnp.sin(ac)], axis=-1)
        reps = width // cos.shape[-1]
        return jnp.tile(cos, (1, reps)), jnp.tile(sin, (1, reps))

    cr, sr = one(RET_DK // 2, 128)
    cd, sd = one(DIFF_HD // 2, 128)
    tab = jnp.stack([cr, sr, cd, sd])
    ident = jnp.stack([jnp.ones((tm, 128), F32), jnp.zeros((tm, 128), F32)] * 2)
    return jnp.concatenate([tab, ident], axis=1)


def _ret_kernel(lg_ref, qf_ref, kf_ref, vf_ref, qb_ref, kb_ref, vb_ref, of_ref, ob_ref, s_ref):
    C = RET_CHUNK

    @pl.when(pl.program_id(1) == 0)
    def _():
        s_ref[...] = jnp.zeros_like(s_ref)

    ii = lax.broadcasted_iota(I32, (C, C), 0).astype(F32)
    jj = lax.broadcasted_iota(I32, (C, C), 1).astype(F32)
    ri = lax.broadcasted_iota(I32, (C, RET_DK), 0).astype(F32)

    for h in range(RET_HEADS):
        qs = slice(h * RET_DK, (h + 1) * RET_DK)
        vs = slice(h * RET_DV, (h + 1) * RET_DV)
        for d, (q_ref, k_ref, v_ref, o_ref) in enumerate(((qf_ref, kf_ref, vf_ref, of_ref),
                                                         (qb_ref, kb_ref, vb_ref, ob_ref))):
            lg = lg_ref[d, h]
            q, k, v = q_ref[:, qs], k_ref[:, qs], v_ref[:, vs]
            if d == 0:
                dist = ii - jj
                q_dec = jnp.exp(lg * (ri + 1.0))
                k_dec = jnp.exp(lg * (C - 1.0 - ri))
            else:
                dist = jj - ii
                q_dec = jnp.exp(lg * (C - ri))
                k_dec = jnp.exp(lg * ri)
            decay = jnp.where(dist >= 0, jnp.exp(lg * jnp.maximum(dist, 0.0)), 0.0)
            att = (_dot_nt(q, k) * decay).astype(BF16)
            s = s_ref[d, h]
            o = _dot(att, v) + _dot((q.astype(F32) * q_dec).astype(BF16), s.astype(BF16))
            s_ref[d, h] = s * jnp.exp(lg * C) + _dot_tn((k.astype(F32) * k_dec).astype(BF16), v)
            o_ref[:, vs] = o.astype(o_ref.dtype)


def _retention(lg, rq, rk, z, dims):
    B, S, C, BS = dims["B"], dims["S"], dims["C"], dims["BS"]
    R = rq.shape[0]
    nc, nl = C // RET_CHUNK, S // RET_CHUNK
    base = BS // RET_CHUNK
    fwd = lambda b, c: jnp.where(c < nc, base + b * nc + c, b * nl + (c - nc))
    bwd = lambda b, c: jnp.where(c < nc, base + b * nc + (nc - 1 - c), b * nl + (nl - 1 - (c - nc)))
    vw = RET_HEADS * RET_DV
    spec = lambda w, col, f: pl.BlockSpec((RET_CHUNK, w), lambda b, c: (f(b, c), col))
    return pl.pallas_call(
        _ret_kernel,
        out_shape=[jax.ShapeDtypeStruct((R, vw), BF16), jax.ShapeDtypeStruct((R, vw), BF16)],
        grid=(B, nc + nl),
        in_specs=[pl.BlockSpec(memory_space=pltpu.SMEM),
                  spec(SEG, 0, fwd), spec(SEG, 0, fwd), spec(vw, C_RV * SEG // vw, fwd),
                  spec(SEG, 0, bwd), spec(SEG, 0, bwd), spec(vw, C_RV * SEG // vw, bwd)],
        out_specs=[spec(vw, 0, fwd), spec(vw, 0, bwd)],
        scratch_shapes=[pltpu.VMEM((2, RET_HEADS, RET_DK, RET_DV), F32)],
        compiler_params=_cparams(("parallel", "arbitrary")),
        name="retention",
    )(lg, rq, rk, z, rq, rk, z)


def _split_halves(q):
    lane = lax.broadcasted_iota(I32, q.shape, 1)
    lo = jnp.where(lane < 64, q, jnp.zeros_like(q))
    hi = jnp.where(lane >= 64, q, jnp.zeros_like(q))
    return jnp.concatenate([lo, hi], axis=0)


def _fold_lanes(x, op):
    acc = x[:, :128]
    for j in range(1, x.shape[1] // 128):
        acc = op(acc, x[:, j * 128:(j + 1) * 128])
    return acc


def _attend_unit(q_fn, sources, lanes, s_ref, e_ref, buf, out):
    chunks = []
    off = 0
    for k_ref, _, n_rows in sources:
        for c0 in range(0, n_rows, KEY_CHUNK):
            n = min(KEY_CHUNK, n_rows - c0)
            chunks.append((k_ref, c0, n, off))
            off += n
    q = q_fn()
    m = None
    for k_ref, r0, n, off in chunks:
        s = _dot_nt(q, k_ref[r0:r0 + n, lanes])
        s_ref[buf, :, off:off + n] = s
        blk = _fold_lanes(s, jnp.maximum)
        m = blk if m is None else jnp.maximum(m, blk)
        yield
    m = m.max(axis=-1, keepdims=True)
    l = jnp.zeros((ROW_BLOCK, 128), F32)
    for _, _, n, off in chunks:
        e = jnp.exp(s_ref[buf, :, off:off + n] - m)
        l = l + _fold_lanes(e, jnp.add)
        e_ref[buf, :, off:off + n] = e.astype(BF16)
        yield
    acc, off = None, 0
    for _, v_ref, n_rows in sources:
        t = _dot(e_ref[buf, :, off:off + n_rows], v_ref[:, lanes])
        acc = t if acc is None else acc + t
        off += n_rows
    out.append((acc, l.sum(axis=-1, keepdims=True)))
    yield


def _run_interleaved(units, n_a, n_b):
    for _ in range(n_a):
        next(units[0])
    for i, u in enumerate(units):
        nxt = units[i + 1] if i + 1 < len(units) else None
        a = 0
        for b in range(n_b):
            next(u, None)
            while nxt is not None and a < (b + 1) * n_a // n_b:
                next(nxt)
                a += 1


def _diff_kernel(lam_ref, q_ref, *refs, has_latent):
    if has_latent:
        kl_ref, vl_ref, kc_ref, vc_ref, o_ref, s_ref, e_ref = refs
        sources = ((kl_ref, vl_ref, kl_ref.shape[0]), (kc_ref, vc_ref, kc_ref.shape[0]))
    else:
        kc_ref, vc_ref, o_ref, s_ref, e_ref = refs
        sources = ((kc_ref, vc_ref, kc_ref.shape[0]),)
    lam = lam_ref[0, 0]
    tq = q_ref.shape[0]
    n_chunks = sum(-(-n_rows // KEY_CHUNK) for _, _, n_rows in sources)
    out, units = [], []
    for h in range(DIFF_HEADS):
        hs = slice(h * 128, (h + 1) * 128)
        for rb in range(0, 2 * tq, ROW_BLOCK):
            def q_fn(hs=hs, rb=rb):
                q = q_ref[rb % tq:rb % tq + ROW_BLOCK, hs]
                lane = lax.broadcasted_iota(I32, q.shape, 1)
                return jnp.where((lane < 64) if rb < tq else (lane >= 64), q, jnp.zeros_like(q))
            units.append(_attend_unit(q_fn, sources, hs, s_ref, e_ref, len(units) % 2, out))
    _run_interleaved(units, n_chunks, n_chunks + 1)
    per_head = 2 * tq // ROW_BLOCK
    for h in range(DIFF_HEADS):
        res = out[h * per_head:(h + 1) * per_head]
        acc = jnp.concatenate([r[0] for r in res], axis=0)
        l = jnp.concatenate([r[1] for r in res], axis=0)
        o = acc[:tq] * (1.0 / l[:tq]) - acc[tq:] * (lam / l[tq:])
        o_ref[:, h * 128:(h + 1) * 128] = o.astype(o_ref.dtype)


def _diff_attention(lam, dq, dk, z, dims, latent):
    B, S, C, BS = dims["B"], dims["S"], dims["C"], dims["BS"]
    lat_k = pl.BlockSpec((S, SEG), lambda b, i: (b, 0))
    lat_v = pl.BlockSpec((S, SEG), lambda b, i: (b, C_DV))
    ctx_k = pl.BlockSpec((C, SEG), lambda b, i: (BS // C + b, 0))
    ctx_v = pl.BlockSpec((C, SEG), lambda b, i: (BS // C + b, C_DV))
    if latent:
        tq = _tile(S, 256)
        nq = S // tq
        qmap = omap = lambda b, i: (b * nq + i, 0)
        in_specs = [lat_k, lat_v, ctx_k, ctx_v]
        args = (dk, z, dk, z)
        n_keys = S + C
    else:
        tq, nq = C, 1
        qmap = lambda b, i: (BS // C + b, 0)
        omap = lambda b, i: (b, 0)
        in_specs = [ctx_k, ctx_v]
        args = (dk, z)
        n_keys = C
    assert tq % ROW_BLOCK == 0
    return pl.pallas_call(
        functools.partial(_diff_kernel, has_latent=latent),
        out_shape=jax.ShapeDtypeStruct((B * nq * tq, SEG), BF16),
        grid=(B, nq),
        in_specs=[pl.BlockSpec(memory_space=pltpu.SMEM), pl.BlockSpec((tq, SEG), qmap)] + in_specs,
        out_specs=pl.BlockSpec((tq, SEG), omap),
        scratch_shapes=[pltpu.VMEM((2, ROW_BLOCK, n_keys), F32), pltpu.VMEM((2, ROW_BLOCK, n_keys), BF16)],
        compiler_params=_cparams(("parallel", "arbitrary")),
        name="diff_lat" if latent else "diff_ctx",
    )(lam, dq, *args)


def _na_kernel(q_ref, *refs, has_local, rows):
    if has_local:
        kl_ref, vl_ref, kc_ref, vc_ref, tidx_ref, tile_ref, o_ref = refs
        qb = pl.program_id(1)
        kr0 = jnp.clip(qb * NA_QROWS - (NA_KROWS - NA_QROWS) // 2, 0, rows - NA_KROWS)
        k0 = pl.multiple_of(kr0 * GRID_W, 256)
        nk = NA_KROWS * GRID_W
        last = rows // NA_QROWS - 1
        btype = jnp.where(qb == 0, 0, jnp.where(qb == last, 2, 1))

        def head_bias(h):
            row_blocks = []
            for i in range(NA_QROWS):
                base = (btype * NA_QROWS + i) * (NA_KROWS // 2)
                row_blocks.append(jnp.concatenate(
                    [tile_ref[0, h, tidx_ref[base + jp]] for jp in range(NA_KROWS // 2)], axis=1))
            return jnp.concatenate(row_blocks, axis=0)
    else:
        kc_ref, vc_ref, o_ref = refs
    tq = q_ref.shape[0]
    lane = lax.broadcasted_iota(I32, (tq, 128), 1)
    for p in range(NA_HEADS // 2):
        ps = slice(p * 128, (p + 1) * 128)
        q2 = _split_halves(q_ref[:, ps])
        parts = [_dot_nt(q2, kc_ref[:, ps])]
        vals = [vc_ref[:, ps]]
        if has_local:
            s_loc = _dot_nt(q2, kl_ref[pl.ds(k0, nk), ps])
            bias = jnp.concatenate([head_bias(2 * p), head_bias(2 * p + 1)], axis=0).astype(F32)
            parts.append(s_loc + bias)
            vals.append(vl_ref[pl.ds(k0, nk), ps])
        m = parts[0].max(axis=-1, keepdims=True)
        for s in parts[1:]:
            m = jnp.maximum(m, s.max(axis=-1, keepdims=True))
        es = [jnp.exp(s - m) for s in parts]
        den = es[0].sum(axis=-1, keepdims=True)
        for e in es[1:]:
            den = den + e.sum(axis=-1, keepdims=True)
        o = None
        for e, v in zip(es, vals):
            t = _dot(e.astype(BF16), v)
            o = t if o is None else o + t
        o = o * (1.0 / den)
        o_ref[:, ps] = jnp.where(lane < 64, o[:tq], o[tq:]).astype(o_ref.dtype)


def _na_attention(nq_, nk_, z, bias, layer, dims, latent):
    B, S, C, BS = dims["B"], dims["S"], dims["C"], dims["BS"]
    ctx_k = pl.BlockSpec((C, SEG), lambda b, i: (BS // C + b, 0))
    ctx_v = pl.BlockSpec((C, SEG), lambda b, i: (BS // C + b, C_NV))
    if latent:
        tq = NA_QROWS * GRID_W
        nq = S // tq
        qmap = omap = lambda b, i: (b * nq + i, 0)
        tiles, tidx = bias
        in_specs = [pl.BlockSpec((S, SEG), lambda b, i: (b, 0)),
                    pl.BlockSpec((S, SEG), lambda b, i: (b, C_NV)),
                    ctx_k, ctx_v,
                    pl.BlockSpec(memory_space=pltpu.SMEM),
                    pl.BlockSpec((1,) + tiles.shape[1:], lambda b, i: (layer, 0, 0, 0, 0))]
        args = (nk_, z, nk_, z, tidx, tiles)
    else:
        tq, nq = C, 1
        qmap = lambda b, i: (BS // C + b, 0)
        omap = lambda b, i: (b, 0)
        in_specs = [ctx_k, ctx_v]
        args = (nk_, z)
    return pl.pallas_call(
        functools.partial(_na_kernel, has_local=latent, rows=S // GRID_W),
        out_shape=jax.ShapeDtypeStruct((B * nq * tq, SEG), BF16),
        grid=(B, nq),
        in_specs=[pl.BlockSpec((tq, SEG), qmap)] + in_specs,
        out_specs=pl.BlockSpec((tq, SEG), omap),
        compiler_params=_cparams(("parallel", "arbitrary")),
        name="na_lat" if latent else "na_ctx",
    )(nq_, *args)


def _na_bias_tables(rpb, S):
    rows = S // GRID_W
    nqb = rows // NA_QROWS
    n_off = 2 * NA_KR - 1
    i = np.arange(NA_QROWS)[:, None]
    j = np.arange(NA_KROWS)[None, :]
    code = np.zeros((3, NA_QROWS, NA_KROWS), np.int64)
    for t, qb in enumerate((0, min(1, nqb - 1), nqb - 1)):
        r = qb * NA_QROWS + i
        kr0 = np.clip(qb * NA_QROWS - (NA_KROWS - NA_QROWS) // 2, 0, rows - NA_KROWS)
        kr = kr0 + j
        r0 = np.clip(r - NA_KR // 2, 0, rows - NA_KR)
        ok = (kr >= r0) & (kr < r0 + NA_KR)
        code[t] = np.where(ok, kr - r + NA_KR - 1, n_off)
    pairs = code.reshape(3, NA_QROWS, NA_KROWS // 2, 2)
    uniq, inverse = np.unique(pairs.reshape(-1, 2), axis=0, return_inverse=True)
    cq = np.arange(GRID_W)[:, None]
    ck = np.arange(GRID_W)[None, :]
    c0 = np.clip(cq - NA_KW // 2, 0, GRID_W - NA_KW)
    okc = (ck >= c0) & (ck < c0 + NA_KW)
    offc = np.clip(ck - cq + NA_KW - 1, 0, 2 * NA_KW - 2)
    sel_c = np.zeros((GRID_W, GRID_W, 2 * NA_KW - 1), np.float32)
    sel_c[np.broadcast_to(cq, okc.shape)[okc], np.broadcast_to(ck, okc.shape)[okc], offc[okc]] = 1.0
    planes = jnp.einsum("lhab,qkb->lhaqk", rpb.astype(F32), jnp.asarray(sel_c), precision=lax.Precision.HIGHEST)
    planes = jnp.where(jnp.asarray(okc)[None, None, None], planes, NEG)
    planes = jnp.concatenate([planes, jnp.full_like(planes[:, :, :1], NEG)], axis=2)
    tiles = jnp.concatenate([planes[:, :, uniq[:, 0]], planes[:, :, uniq[:, 1]]], axis=-1)
    return tiles.astype(BF16), jnp.asarray(inverse.reshape(-1), I32)


def _head_norm(x, width, gain):
    outs = []
    for h in range(x.shape[-1] // width):
        xh = x[:, h * width:(h + 1) * width]
        outs.append(xh * lax.rsqrt(jnp.mean(xh * xh, axis=-1, keepdims=True) + EPS))
    return jnp.concatenate(outs, axis=-1) * gain


def _merge_kernel(x_ref, gate_ref, rf_ref, rb_ref, rg_ref, *refs, diff_scale, n_lat_tiles):
    if n_lat_tiles is None:
        od_ref, on_ref = refs[:2]
        od, on = od_ref[...], on_ref[...]
    else:
        odl_ref, onl_ref, odc_ref, onc_ref = refs[:4]
        is_lat = pl.program_id(0) < n_lat_tiles
        od = jnp.where(is_lat, odl_ref[...], odc_ref[...])
        on = jnp.where(is_lat, onl_ref[...], onc_ref[...])
    g1_ref, g2_ref, g3_ref, retg_ref, dg_ref, wr_ref, wd_ref, wn_ref, wo_ref, o_ref = refs[-10:]
    r = _head_norm(rf_ref[...].astype(F32) + rb_ref[...].astype(F32), RET_DV, retg_ref[...])
    rg = rg_ref[...].astype(F32)
    y_ret = _dot((rg * _sigmoid(rg) * r).astype(BF16), wr_ref[0])
    d = _head_norm(od.astype(F32), DIFF_DV, dg_ref[...]) * diff_scale
    y_diff = _dot(d.astype(BF16), wd_ref[0])
    y_na = _dot(on, wn_ref[0])
    m = (_sigmoid(g1_ref[...].astype(F32)) * y_ret + _sigmoid(g2_ref[...].astype(F32)) * y_diff
         + _sigmoid(g3_ref[...].astype(F32)) * y_na)
    y = _dot(m.astype(BF16), wo_ref[0])
    o_ref[...] = x_ref[...] + gate_ref[0] * y


def _merge(x, gate, rf, rb, z, attn, ret_g, d_g, wr, wd, wn, wo, layer, diff_scale, dims, rows):
    D = x.shape[1]
    tm = _tile(dims["BC"], 256)
    nlt, tpb, B = dims["BS"] // tm, dims["S"] // tm, dims["B"]
    rw = RET_HEADS * RET_DV
    gcol = C_GATES * SEG // D
    row = lambda w, col=0: pl.BlockSpec((tm, w), lambda i: (i, col))
    lat = pl.BlockSpec((tm, SEG), lambda i: (jnp.minimum(i, nlt - 1), 0))
    ctx = pl.BlockSpec((tm, SEG), lambda i: (jnp.maximum(i - nlt, 0), 0))
    const = lambda a: pl.BlockSpec(a.shape, lambda i: (0,) * a.ndim)
    wspec = lambda a: pl.BlockSpec((1,) + a.shape[1:], lambda i: (layer, 0, 0), pipeline_mode=pl.Buffered(1))
    with_ctx = len(attn) == 4
    return pl.pallas_call(
        functools.partial(_merge_kernel, diff_scale=diff_scale, n_lat_tiles=nlt if with_ctx else None),
        out_shape=jax.ShapeDtypeStruct((rows, D), F32),
        grid=(rows // tm,),
        in_specs=[row(D), pl.BlockSpec((1, 1, D), lambda i: (_mod_index(i, nlt, tpb, B), 0, 0)),
                  row(rw), row(rw), row(rw, C_RG * SEG // rw)]
                 + ([lat, lat, ctx, ctx] if with_ctx else [row(SEG), row(SEG)])
                 + [row(D, gcol), row(D, gcol + 1), row(D, gcol + 2),
                    const(ret_g), const(d_g), wspec(wr), wspec(wd), wspec(wn), wspec(wo)],
        out_specs=row(D),
        compiler_params=_cparams(("parallel",)),
        name="merge",
    )(x, gate, rf, rb, z, *attn, z, z, z, ret_g, d_g, wr, wd, wn, wo)


def _to_token_tiles(ref, x):
    rows, width = x.shape
    n = width // 128
    for c in range(n):
        ref[pl.ds(c, rows, stride=n), :] = x[:, c * 128:(c + 1) * 128]


def _from_token_tiles(ref, rows):
    n = ref.shape[0] // rows
    return jnp.concatenate([ref[pl.ds(c, rows, stride=n), :] for c in range(n)], axis=1)


def _router_kernel(x_ref, g_ref, sh_ref, sc_ref, wrt_ref, br_ref, h_ref, idx_ref, wts_ref, cnt_ref, run_ref):
    i = pl.program_id(0)
    tm = x_ref.shape[0]
    E = N_EXPERTS

    @pl.when(i == 0)
    def _():
        run_ref[...] = jnp.zeros_like(run_ref)

    h = _modnorm(x_ref[...], g_ref[...], sh_ref[0], sc_ref[0])
    _to_token_tiles(h_ref, h)
    h_hi = h.astype(BF16)
    h_lo = (h - h_hi.astype(F32)).astype(BF16)
    w = wrt_ref[...]
    w_hi = w.astype(BF16)
    w_lo = (w - w_hi.astype(F32)).astype(BF16)
    logits = _dot_nt(w_hi, h_hi) + _dot_nt(w_hi, h_lo) + _dot_nt(w_lo, h_hi)
    s = _sigmoid(logits)
    sel = s + br_ref[...]

    best, gi = None, None
    for g in range(N_GROUPS):
        r = [sel[g * EXPERTS_PER_GROUP + k:g * EXPERTS_PER_GROUP + k + 1] for k in range(EXPERTS_PER_GROUP)]
        top2 = None
        for a in range(EXPERTS_PER_GROUP):
            for b in range(a + 1, EXPERTS_PER_GROUP):
                pair = r[a] + r[b]
                top2 = pair if top2 is None else jnp.maximum(top2, pair)
        if best is None:
            best, gi = top2, jnp.zeros(top2.shape, I32)
        else:
            upd = top2 > best
            gi = jnp.where(upd, g, gi)
            best = jnp.where(upd, top2, best)

    eid = lax.broadcasted_iota(I32, (E, tm), 0)
    masked = jnp.where(eid // EXPERTS_PER_GROUP == gi, sel, -jnp.inf)
    m1 = masked.max(axis=0, keepdims=True)
    i1 = jnp.where(masked == m1, eid, E).min(axis=0, keepdims=True)
    masked2 = jnp.where(eid == i1, -jnp.inf, masked)
    m2 = masked2.max(axis=0, keepdims=True)
    i2 = jnp.where(masked2 == m2, eid, E).min(axis=0, keepdims=True)
    oh1 = eid == i1
    oh2 = eid == i2
    w1 = jnp.where(oh1, s, 0.0).sum(axis=0, keepdims=True)
    w2 = jnp.where(oh2, s, 0.0).sum(axis=0, keepdims=True)
    tot = w1 + w2

    oh = (oh1 | oh2).astype(F32)
    before = (lax.broadcasted_iota(I32, (tm, tm), 0) < lax.broadcasted_iota(I32, (tm, tm), 1)).astype(BF16)
    prefix = _dot(oh.astype(BF16), before) + run_ref[:, 0:1]
    rank1 = jnp.where(oh1, prefix, 0.0).sum(axis=0, keepdims=True)
    rank2 = jnp.where(oh2, prefix, 0.0).sum(axis=0, keepdims=True)
    run_ref[...] = run_ref[...] + oh.sum(axis=1, keepdims=True)

    idx_ref[...] = jnp.zeros_like(idx_ref)
    wts_ref[...] = jnp.zeros_like(wts_ref)
    for k, v in enumerate((i1, i2, rank1.astype(I32), rank2.astype(I32))):
        idx_ref[0, k:k + 1, :] = v
    wts_ref[0, 0:1, :] = w1 / tot
    wts_ref[0, 1:2, :] = w2 / tot
    cnt_ref[...] = run_ref[...]


def _router(x, g, shift, scale, wrt, br, dims, rows):
    D = x.shape[1]
    tm = _tile(dims["BC"], 256)
    nt = rows // tm
    nlt, tpb, B = dims["BS"] // tm, dims["S"] // tm, dims["B"]
    mod = lambda i: (_mod_index(i, nlt, tpb, B), 0, 0)
    n = D // 128
    return pl.pallas_call(
        _router_kernel,
        out_shape=[jax.ShapeDtypeStruct((rows * n, 128), F32),
                   jax.ShapeDtypeStruct((nt, 8, tm), I32),
                   jax.ShapeDtypeStruct((nt, 8, tm), F32),
                   jax.ShapeDtypeStruct((N_EXPERTS, 128), F32)],
        grid=(nt,),
        in_specs=[pl.BlockSpec((tm, D), lambda i: (i, 0)),
                  pl.BlockSpec((1, D), lambda i: (0, 0)),
                  pl.BlockSpec((1, 1, D), mod), pl.BlockSpec((1, 1, D), mod),
                  pl.BlockSpec((N_EXPERTS, D), lambda i: (0, 0)),
                  pl.BlockSpec((N_EXPERTS, 1), lambda i: (0, 0))],
        out_specs=[pl.BlockSpec((tm * n, 128), lambda i: (i, 0)),
                   pl.BlockSpec((1, 8, tm), lambda i: (i, 0, 0)),
                   pl.BlockSpec((1, 8, tm), lambda i: (i, 0, 0)),
                   pl.BlockSpec((N_EXPERTS, 128), lambda i: (0, 0))],
        scratch_shapes=[pltpu.VMEM((N_EXPERTS, 128), F32)],
        compiler_params=_cparams(("arbitrary",)),
        name="router",
    )(x, g, shift, scale, wrt, br)


GATHER_UNROLL = 16


def _token_copy(r, n, idx_ref, src_hbm, dst_ref, sem):
    src = pl.multiple_of(idx_ref[0, 0, r] * n, n)
    dst = pl.multiple_of(r * n, n)
    return pltpu.make_async_copy(src_hbm.at[pl.ds(src, n)], dst_ref.at[pl.ds(dst, n)], sem)


def _start_gather(n_tok, n, idx_ref, src_hbm, dst_ref, sem):
    def body(g, carry):
        for u in range(GATHER_UNROLL):
            _token_copy(g * GATHER_UNROLL + u, n, idx_ref, src_hbm, dst_ref, sem).start(priority=u % 2)
        return carry

    lax.fori_loop(0, n_tok // GATHER_UNROLL, body, 0)


def _wait_gather(n_tok, n, idx_ref, src_hbm, dst_ref, sem):
    def body(g, carry):
        for u in range(GATHER_UNROLL):
            _token_copy(g * GATHER_UNROLL + u, n, idx_ref, src_hbm, dst_ref, sem).wait()
        return carry

    lax.fori_loop(0, n_tok // GATHER_UNROLL, body, 0)


def _pipelined_gathers(streams, n_tok, n):
    t = pl.program_id(0)
    last = pl.num_programs(0) - 1
    slot = lax.rem(t, 2)

    @pl.when(t == 0)
    def _():
        for cur, _, src, buf, sem in streams:
            _start_gather(n_tok, n, cur, src, buf.at[0], sem.at[0])

    @pl.when(t < last)
    def _():
        for _, nxt, src, buf, sem in streams:
            _start_gather(n_tok, n, nxt, src, buf.at[1 - slot], sem.at[1 - slot])

    for cur, _, src, buf, sem in streams:
        _wait_gather(n_tok, n, cur, src, buf.at[slot], sem.at[slot])
    return slot


def _expert_kernel(te_ref, cur_ref, nxt_ref, h_hbm, wg_ref, wu_ref, wd_ref, o_ref, x_buf, sem, *, rows, n):
    slot = _pipelined_gathers(((cur_ref, nxt_ref, h_hbm, x_buf, sem),), rows, n)
    x = _from_token_tiles(x_buf.at[slot], rows).astype(BF16)
    gate = _dot(x, wg_ref[0, 0])
    a = gate * _sigmoid(gate) * _dot(x, wu_ref[0, 0])
    _to_token_tiles(o_ref, _dot(a.astype(BF16), wd_ref[0, 0]))


def _experts(tile_expert, src_rows, h, wg, wu, wd, layer, tmE):
    D, De = wg.shape[2:]
    n = D // 128
    P = src_rows.shape[0]
    nt = P // tmE
    wspec = lambda a, b: pl.BlockSpec((1, 1, a, b), lambda t, te: (layer, te[t], 0, 0))
    idx = src_rows.reshape(nt, 1, tmE)
    cur = pl.BlockSpec((1, 1, tmE), lambda t, te: (t, 0, 0), memory_space=pltpu.SMEM)
    nxt = pl.BlockSpec((1, 1, tmE), lambda t, te: (jnp.minimum(t + 1, nt - 1), 0, 0), memory_space=pltpu.SMEM)
    return pl.pallas_call(
        functools.partial(_expert_kernel, rows=tmE, n=n),
        out_shape=jax.ShapeDtypeStruct((P * n, 128), F32),
        grid_spec=pltpu.PrefetchScalarGridSpec(
            num_scalar_prefetch=1,
            grid=(nt,),
            in_specs=[cur, nxt, pl.BlockSpec(memory_space=pl.ANY), wspec(D, De), wspec(D, De), wspec(De, D)],
            out_specs=pl.BlockSpec((tmE * n, 128), lambda t, te: (t, 0)),
            scratch_shapes=[pltpu.VMEM((2, tmE * n, 128), F32), pltpu.SemaphoreType.DMA((2,))]),
        compiler_params=_cparams(("arbitrary",)),
        name="experts",
    )(tile_expert, idx, idx, h, wg, wu, wd)


def _combine_kernel(c1_ref, n1_ref, c2_ref, n2_ref, x_ref, gate_ref, w_ref, ye_hbm, o_ref,
                    b1_ref, b2_ref, sem1, sem2):
    rows, width = x_ref.shape
    n = width // 128
    slot = _pipelined_gathers(((c1_ref, n1_ref, ye_hbm, b1_ref, sem1), (c2_ref, n2_ref, ye_hbm, b2_ref, sem2)),
                              rows, n)
    w = w_ref[...]
    y = (w[:, 0:1] * _from_token_tiles(b1_ref.at[slot], rows)
         + w[:, 1:2] * _from_token_tiles(b2_ref.at[slot], rows))
    o_ref[...] = x_ref[...] + gate_ref[0] * y


def _combine(pos1, pos2, x, gate, w, ye, dims, rows):
    D = x.shape[1]
    n = D // 128
    tc = _tile(dims["BC"], 256)
    nt = rows // tc
    nlt, tpb, B = dims["BS"] // tc, dims["S"] // tc, dims["B"]
    cur = pl.BlockSpec((1, 1, tc), lambda i: (i, 0, 0), memory_space=pltpu.SMEM)
    nxt = pl.BlockSpec((1, 1, tc), lambda i: (jnp.minimum(i + 1, nt - 1), 0, 0), memory_space=pltpu.SMEM)
    p1, p2 = pos1.reshape(nt, 1, tc), pos2.reshape(nt, 1, tc)
    return pl.pallas_call(
        _combine_kernel,
        out_shape=jax.ShapeDtypeStruct((rows, D), F32),
        grid=(nt,),
        in_specs=[cur, nxt, cur, nxt,
                  pl.BlockSpec((tc, D), lambda i: (i, 0)),
                  pl.BlockSpec((1, 1, D), lambda i: (_mod_index(i, nlt, tpb, B), 0, 0)),
                  pl.BlockSpec((tc, 2), lambda i: (i, 0)),
                  pl.BlockSpec(memory_space=pl.ANY)],
        out_specs=pl.BlockSpec((tc, D), lambda i: (i, 0)),
        scratch_shapes=[pltpu.VMEM((2, tc * n, 128), F32), pltpu.VMEM((2, tc * n, 128), F32),
                        pltpu.SemaphoreType.DMA((2,)), pltpu.SemaphoreType.DMA((2,))],
        compiler_params=_cparams(("arbitrary",)),
        name="combine",
    )(p1, p1, p2, p2, x, gate, w, ye)


def _moe(x, g, shift, scale, gate, wrt, br, wg, wu, wd, layer, dims, rows):
    tmE = 256
    h, idx, wts, cnt = _router(x, g, shift, scale, wrt, br, dims, rows)
    nt = idx.shape[0]
    flat = lambda a, k: a[:, k, :].reshape(rows)
    e1, e2, r1, r2 = (flat(idx, k) for k in range(4))
    counts = cnt[:, 0].astype(I32)
    padded = (counts + tmE - 1) // tmE * tmE
    ends = jnp.cumsum(padded)
    starts = ends - padded
    pos1 = starts[e1] + r1
    pos2 = starts[e2] + r2
    P = (2 * rows + N_EXPERTS * tmE) // tmE * tmE
    tok = jnp.arange(rows, dtype=I32)
    src = jnp.zeros((P,), I32).at[jnp.concatenate([pos1, pos2])].set(
        jnp.concatenate([tok, tok]), unique_indices=True, mode="promise_in_bounds")
    tile_start = jnp.arange(P // tmE, dtype=I32) * tmE
    tile_expert = jnp.minimum(jnp.sum(ends[None, :] <= tile_start[:, None], axis=1), N_EXPERTS - 1).astype(I32)
    ye = _experts(tile_expert, src, h, wg, wu, wd, layer, tmE)
    w = jnp.stack([flat(wts, 0), flat(wts, 1)], axis=-1)
    return _combine(pos1, pos2, x, gate, w, ye, dims, rows)


def kernel(x, c, ctx, c_ctx, w_ada, b_ada, norm1_g, norm2_g, w_in, ret_decay_logit, ret_norm_g, diff_q_norm_g, diff_k_norm_g, diff_lambda, diff_norm_g, na_q_norm_g, na_k_norm_g, na_rpb, w_ret_out, w_diff_out, w_na_out, w_out, w_router, b_router, w_exp_gate, w_exp_up, w_exp_down):
    B, S, D = x.shape
    C = ctx.shape[1]
    L = w_in.shape[0]
    BS, BC = B * S, B * C
    assert S % (NA_QROWS * GRID_W) == 0 and S // GRID_W >= NA_KROWS and C % RET_CHUNK == 0
    assert (C_GATES * SEG) % D == 0 and w_in.shape[2] == C_GATES * SEG + 3 * D
    dims = dict(B=B, S=S, C=C, BS=BS, BC=BC)

    n_c = -(-(B + 1) // 8) * 8
    c_all = jnp.concatenate([c, c_ctx[None], jnp.zeros((n_c - B - 1, D), F32)], axis=0)
    mods = _ada(c_all, w_ada, b_ada)[:, :B + 1]
    mod = lambda l, k: mods[l, :, k * D:(k + 1) * D].reshape(B + 1, 1, D)

    xa = jnp.concatenate([x.reshape(BS, D), ctx.reshape(BC, D)], axis=0)

    tm_prep = _tile(BC, 512)
    tables = _rope_tables(S, tm_prep)
    gmat = jnp.asarray(np.kron(np.eye(SEG // 64), np.full((64, 64), 1.0 / 64)), BF16)
    bias = _na_bias_tables(na_rpb, S)
    lg = jax.nn.log_sigmoid(ret_decay_logit.astype(F32))
    lam_p = diff_lambda.astype(F32)
    wrt = w_router.T
    br = b_router.astype(F32).reshape(N_EXPERTS, 1)
    tile8 = lambda v: jnp.tile(v.astype(F32), SEG // v.shape[0])
    row = lambda v: v.astype(F32).reshape(1, -1)
    w_in_b, w_ret_b, w_diff_b, w_na_b, w_out_b, w_gate_b, w_up_b, w_down_b = (
        w.astype(BF16) for w in (w_in, w_ret_out, w_diff_out, w_na_out, w_out, w_exp_gate, w_exp_up, w_exp_down))

    for l in range(L):
        last = l == L - 1
        lam_init = 0.8 - 0.6 * math.exp(-0.3 * l)
        lam = (jnp.exp(jnp.sum(lam_p[l, 0] * lam_p[l, 1])) - jnp.exp(jnp.sum(lam_p[l, 2] * lam_p[l, 3]))
               + lam_init).reshape(1, 1)
        z = _inproj(xa, row(norm1_g[l]), mod(l, 0), mod(l, 1), w_in_b, l, dims)
        gains = jnp.stack([tile8(diff_q_norm_g[l]), tile8(diff_k_norm_g[l]),
                           tile8(na_q_norm_g[l]), tile8(na_k_norm_g[l])] + [jnp.zeros((SEG,), F32)] * 4)
        rq, rk, dq, dk, nq, nk = _prep(z, gains, gmat, tables, dims)

        rf, rb = _retention(lg[l], rq, rk, z, dims)
        attn = (_diff_attention(lam, dq, dk, z, dims, True), _na_attention(nq, nk, z, bias, l, dims, True))
        rows = BS if last else BS + BC
        if not last:
            attn += (_diff_attention(lam, dq, dk, z, dims, False), _na_attention(nq, nk, z, None, l, dims, False))
        x1 = _merge(xa, mod(l, 2), rf, rb, z, attn, row(ret_norm_g[l]), row(diff_norm_g[l]),
                    w_ret_b, w_diff_b, w_na_b, w_out_b, l, 1.0 - lam_init, dims, rows)
        xa = _moe(x1, row(norm2_g[l]), mod(l, 3), mod(l, 4), mod(l, 5), wrt, br,
                  w_gate_b, w_up_b, w_down_b, l, dims, rows)
    return xa.reshape(B, S, D)
```
